```python
import math
import jax, jax.numpy as jnp
from jax import lax
import numpy as np

D_MODEL = 1024
BATCH = 2
SEQ = 16384
DEPTH = 2
DEC_BATCH = 4
DEC_SEQ = 4096
PAST_LEN = 128

HEAD_DIM = 64
N_HEADS = D_MODEL // HEAD_DIM
N_KV_HEADS = N_HEADS // 4
GROUP = N_HEADS // N_KV_HEADS
ATTN_WIDTH = N_HEADS * HEAD_DIM
KV_WIDTH = N_KV_HEADS * HEAD_DIM
ATTN_IN_WIDTH = 2 * ATTN_WIDTH + 2 * KV_WIDTH
WINDOW = 128
BLOCK = 128
NUM_BUCKETS = 32
MAX_DISTANCE = 128
N_FGROUPS = 4
FGROUP_CH = D_MODEL // N_FGROUPS
FOURIER_WIDTH = D_MODEL
RMS_EPS = 1e-6
N_MIXERS = 2
N_ATTN_LAYERS = (DEPTH + 1) // 2
N_FOURIER_LAYERS = DEPTH // 2

kernel_name = "hybrid_window_gqa_fnet_encoder"


def rmsnorm(x, g):
    xf = x.astype(jnp.float32)
    inv = lax.rsqrt(jnp.mean(xf * xf, axis=-1, keepdims=True) + RMS_EPS)
    return (xf * inv * g.astype(jnp.float32)).astype(x.dtype)


def t5_bucket_np(rel):
    half = NUM_BUCKETS // 2
    n = -rel
    ret = (n < 0).astype(np.int32) * half
    n = np.abs(n)
    max_exact = half // 2
    is_small = n < max_exact
    large = max_exact + (np.log(np.maximum(n, 1) / max_exact) / math.log(MAX_DISTANCE / max_exact)
                         * (half - max_exact)).astype(np.int32)
    large = np.minimum(large, half - 1)
    return (ret + np.where(is_small, n, large)).astype(np.int32)


def band_windows(t, nb):
    b = t.shape[0]
    tp = jnp.pad(t, ((0, 0), (BLOCK, BLOCK), (0, 0), (0, 0)))
    tb = tp.reshape(b, nb + 2, BLOCK, t.shape[2], t.shape[3])
    return jnp.concatenate([tb[:, :-2], tb[:, 1:-1], tb[:, 2:]], axis=2)


def banded_gqa_with_sink(q, k, v, rel_bias, sink):
    b, s = q.shape[0], q.shape[1]
    nb = s // BLOCK
    qb = q.reshape(b, nb, BLOCK, N_KV_HEADS, GROUP, HEAD_DIM)
    kw = band_windows(k, nb)
    vw = band_windows(v, nb)
    qi = np.arange(BLOCK)[:, None]
    kj = np.arange(3 * BLOCK)[None, :]
    rel = kj - BLOCK - qi
    band = np.abs(rel) <= WINDOW
    kpos = (np.arange(nb)[:, None] - 1) * BLOCK + np.arange(3 * BLOCK)[None, :]
    valid = (kpos >= 0) & (kpos < s)
    mask = jnp.asarray(band[None, :, :] & valid[:, None, :])
    bucket = jnp.asarray(t5_bucket_np(rel))
    bias = rel_bias.astype(jnp.float32)[bucket]
    bias = jnp.transpose(bias, (2, 0, 1)).reshape(N_KV_HEADS, GROUP, BLOCK, 3 * BLOCK)
    scale = HEAD_DIM ** -0.5
    scores = jnp.einsum('bnqhgd,bnkhd->bnhgqk', qb, kw).astype(jnp.float32) * scale + bias
    scores = jnp.where(mask[None, :, None, None, :, :], scores, -jnp.inf)
    sk = sink.astype(jnp.float32).reshape(N_KV_HEADS, GROUP)[None, None, :, :, None, None]
    m = jnp.maximum(jnp.max(scores, axis=-1, keepdims=True), sk)
    p = jnp.exp(scores - m)
    denom = jnp.sum(p, axis=-1, keepdims=True) + jnp.exp(sk - m)
    p = p / denom
    o = jnp.einsum('bnhgqk,bnkhd->bnqhgd', p, vw.astype(jnp.float32))
    return o.reshape(b, s, ATTN_WIDTH).astype(q.dtype)


def attention_layer(x, g, w_in, w_out, sink, rel_bias):
    b, s, _ = x.shape
    h = rmsnorm(x, g)
    z = h @ w_in
    q = z[..., :ATTN_WIDTH].reshape(b, s, N_HEADS, HEAD_DIM)
    k = z[..., ATTN_WIDTH:ATTN_WIDTH + KV_WIDTH].reshape(b, s, N_KV_HEADS, HEAD_DIM)
    v = z[..., ATTN_WIDTH + KV_WIDTH:ATTN_WIDTH + 2 * KV_WIDTH].reshape(b, s, N_KV_HEADS, HEAD_DIM)
    gate = z[..., ATTN_WIDTH + 2 * KV_WIDTH:]
    o = banded_gqa_with_sink(q, k, v, rel_bias, sink)
    return x + (o * jax.nn.silu(gate)) @ w_out


def fourier_layer(x, g, w_gate, w_out):
    b, s, d = x.shape
    h = rmsnorm(x, g)
    hg = h.astype(jnp.float32).reshape(b, s, N_FGROUPS, FGROUP_CH)
    f = jnp.fft.fftn(hg, axes=(1, 3), norm="ortho").real
    f = f.reshape(b, s, d).astype(x.dtype)
    gate = h @ w_gate
    return x + (f * jax.nn.silu(gate)) @ w_out


def trunk(x, rel_bias, attn_norm, attn_w_in, attn_w_out, attn_sink,
          fourier_norm, fourier_w_gate, fourier_w_out, final_norm):
    for i in range(DEPTH):
        j = i // N_MIXERS
        if i % N_MIXERS == 0:
            x = attention_layer(x, attn_norm[j], attn_w_in[j], attn_w_out[j], attn_sink[j], rel_bias)
        else:
            x = fourier_layer(x, fourier_norm[j], fourier_w_gate[j], fourier_w_out[j])
    return rmsnorm(x, final_norm)


def setup_inputs(seed: int = 0) -> dict:
    key = jax.random.key(seed)
    ks = jax.random.split(key, 12)
    f32 = jnp.float32
    return {
        "x_prompt": jax.random.normal(ks[0], (BATCH, SEQ, D_MODEL), f32),
        "x_sample": jax.random.normal(ks[1], (DEC_BATCH, DEC_SEQ, D_MODEL), f32),
        "rel_bias": 0.5 * jax.random.normal(ks[2], (NUM_BUCKETS, N_HEADS), f32),
        "attn_norm": 1.0 + 0.02 * jax.random.normal(ks[3], (N_ATTN_LAYERS, D_MODEL), f32),
        "attn_w_in": jax.random.normal(ks[4], (N_ATTN_LAYERS, D_MODEL, ATTN_IN_WIDTH), f32) * D_MODEL ** -0.5,
        "attn_w_out": jax.random.normal(ks[5], (N_ATTN_LAYERS, ATTN_WIDTH, D_MODEL), f32) * ATTN_WIDTH ** -0.5,
        "attn_sink": 0.5 * jax.random.normal(ks[6], (N_ATTN_LAYERS, N_HEADS), f32),
        "fourier_norm": 1.0 + 0.02 * jax.random.normal(ks[7], (N_FOURIER_LAYERS, D_MODEL), f32),
        "fourier_w_gate": jax.random.normal(ks[8], (N_FOURIER_LAYERS, D_MODEL, FOURIER_WIDTH), f32) * D_MODEL ** -0.5,
        "fourier_w_out": jax.random.normal(ks[9], (N_FOURIER_LAYERS, FOURIER_WIDTH, D_MODEL), f32) * FOURIER_WIDTH ** -0.5,
        "final_norm": 1.0 + 0.02 * jax.random.normal(ks[10], (D_MODEL,), f32),
    }


def reference(x_prompt, x_sample, rel_bias, attn_norm, attn_w_in, attn_w_out, attn_sink,
              fourier_norm, fourier_w_gate, fourier_w_out, final_norm):
    y_prompt = trunk(x_prompt, rel_bias, attn_norm, attn_w_in, attn_w_out, attn_sink,
                     fourier_norm, fourier_w_gate, fourier_w_out, final_norm)
    y_sample = trunk(x_sample, rel_bias, attn_norm, attn_w_in, attn_w_out, attn_sink,
                     fourier_norm, fourier_w_gate, fourier_w_out, final_norm)
    return (y_prompt, y_sample)
```

```python
import functools
import math

import jax
import jax.numpy as jnp
import numpy as np
from jax import lax
from jax.experimental import pallas as pl
from jax.experimental.pallas import tpu as pltpu

D_MODEL = 1024
HEAD_DIM = 64
N_HEADS = 16
N_KV_HEADS = 4
GROUP = 4
KV_WIDTH = N_KV_HEADS * HEAD_DIM
WINDOW = 128
BLOCK = 128
NUM_BUCKETS = 32
MAX_DISTANCE = 128
N_FGROUPS = 4
FGROUP_CH = 256
RMS_EPS = 1e-6
MASK_VALUE = -1e30

DFT_N1 = 128
ROW_TILE = 512
STAGE1_COLS = 2048
VMEM_LIMIT = 48 * 1024 * 1024

BF16 = jnp.bfloat16
F32 = jnp.float32


def _params(*sem):
    return pltpu.CompilerParams(dimension_semantics=sem, vmem_limit_bytes=VMEM_LIMIT)


def _rms_scale(x, g):
    inv = lax.rsqrt(jnp.mean(x * x, axis=-1, keepdims=True) + RMS_EPS)
    return x * inv * g


def _silu(z):
    return z * (1.0 / (1.0 + jnp.exp(-z)))


def _t5_bucket_np(rel):
    half = NUM_BUCKETS // 2
    n = -rel
    ret = (n < 0).astype(np.int32) * half
    n = np.abs(n)
    max_exact = half // 2
    is_small = n < max_exact
    large = max_exact + (np.log(np.maximum(n, 1) / max_exact) / math.log(MAX_DISTANCE / max_exact)
                         * (half - max_exact)).astype(np.int32)
    large = np.minimum(large, half - 1)
    return (ret + np.where(is_small, n, large)).astype(np.int32)


@functools.lru_cache(maxsize=None)
def _band_tables():
    qi = np.arange(BLOCK)[:, None]
    kj = np.arange(3 * BLOCK)[None, :]
    rel = kj - BLOCK - qi
    band = np.abs(rel) <= WINDOW
    return _t5_bucket_np(rel), np.where(band, 0.0, MASK_VALUE).astype(np.float32)


@functools.lru_cache(maxsize=None)
def _channel_dft_table():
    c = np.arange(FGROUP_CH)
    ang = 2.0 * np.pi * ((c[:, None] * c[None, :]) % FGROUP_CH) / FGROUP_CH
    return (np.concatenate([np.cos(ang), np.sin(ang)], axis=1) / 16.0).astype(np.float32)


@functools.lru_cache(maxsize=None)
def _stage1_table(seq):
    k = np.arange(DFT_N1)
    ang = 2.0 * np.pi * ((k[:, None] * k[None, :]) % DFT_N1) / DFT_N1
    c, s = np.cos(ang), np.sin(ang)
    w = np.block([[c, -s], [-s, -c]]) / math.sqrt(seq)
    return w.astype(np.float32)


@functools.lru_cache(maxsize=None)
def _stage2_table(seq):
    n2_len = seq // DFT_N1
    k1 = np.arange(DFT_N1)[:, None, None]
    k2 = np.arange(n2_len)[None, :, None]
    n2 = np.arange(n2_len)[None, None, :]
    ang = 2.0 * np.pi * (((k1 + DFT_N1 * k2) * n2) % seq) / seq
    return np.concatenate([np.cos(ang), np.sin(ang)], axis=2).astype(np.float32)


def _inproj_kernel(x_ref, g_ref, wq_ref, wkv_ref, wg_ref, q_ref, kv_ref, sg_ref):
    h = _rms_scale(x_ref[...], g_ref[...]).astype(BF16)
    q = jnp.dot(h, wq_ref[...], preferred_element_type=F32)
    q_ref[...] = (q * (HEAD_DIM ** -0.5)).astype(BF16)
    kv_ref[...] = jnp.dot(h, wkv_ref[...], preferred_element_type=F32).astype(BF16)
    gate = jnp.dot(h, wg_ref[...], preferred_element_type=F32)
    sg_ref[...] = _silu(gate).astype(BF16)


def _inproj(x2, g, wq, wkv, wg):
    t = x2.shape[0]
    row = lambda i: (i, 0)
    const = lambda i: (0, 0)
    return pl.pallas_call(
        _inproj_kernel,
        grid=(t // ROW_TILE,),
        in_specs=[
            pl.BlockSpec((ROW_TILE, D_MODEL), row),
            pl.BlockSpec((1, D_MODEL), const),
            pl.BlockSpec((D_MODEL, D_MODEL), const),
            pl.BlockSpec((D_MODEL, 2 * KV_WIDTH), const),
            pl.BlockSpec((D_MODEL, D_MODEL), const),
        ],
        out_specs=[
            pl.BlockSpec((ROW_TILE, D_MODEL), row),
            pl.BlockSpec((ROW_TILE, 2 * KV_WIDTH), row),
            pl.BlockSpec((ROW_TILE, D_MODEL), row),
        ],
        out_shape=[
            jax.ShapeDtypeStruct((t, D_MODEL), BF16),
            jax.ShapeDtypeStruct((t, 2 * KV_WIDTH), BF16),
            jax.ShapeDtypeStruct((t, D_MODEL), BF16),
        ],
        compiler_params=_params("parallel"),
        name="inproj",
    )(x2, g, wq, wkv, wg)


def _attn_kernel(sink_ref, q_ref, sg_ref, x_ref, kvp_ref, kvc_ref, kvn_ref, bias_ref, wo_ref,
                 out_ref, kvw_ref, o_ref):
    tq = q_ref.shape[1]
    blocks_per_tile = tq // BLOCK
    n_blocks = pl.num_programs(1) * blocks_per_tile
    tile = pl.program_id(1)

    kvw_ref[0:BLOCK, :] = kvp_ref[0]
    kvw_ref[BLOCK:BLOCK + tq, :] = kvc_ref[0]
    kvw_ref[BLOCK + tq:, :] = kvn_ref[0]

    col = lax.broadcasted_iota(jnp.int32, (1, 3 * BLOCK), 1)

    def block_body(j, carry):
        gb = tile * blocks_per_tile + j
        outside = jnp.logical_or(jnp.logical_and(gb == 0, col < BLOCK),
                                 jnp.logical_and(gb == n_blocks - 1, col >= 2 * BLOCK))
        edge = jnp.where(outside, MASK_VALUE, 0.0)
        r0 = pl.multiple_of(j * BLOCK, BLOCK)
        for hh in range(N_KV_HEADS):
            k = kvw_ref[pl.ds(r0, 3 * BLOCK), hh * HEAD_DIM:(hh + 1) * HEAD_DIM]
            v = kvw_ref[pl.ds(r0, 3 * BLOCK), KV_WIDTH + hh * HEAD_DIM:KV_WIDTH + (hh + 1) * HEAD_DIM]
            heads = [hh * GROUP + g for g in range(GROUP)]
            qs = jnp.concatenate(
                [q_ref[0, pl.ds(r0, BLOCK), hd * HEAD_DIM:(hd + 1) * HEAD_DIM] for hd in heads], axis=0)
            s_all = lax.dot_general(qs, k, (((1,), (1,)), ((), ())), preferred_element_type=F32)
            ps, ls = [], []
            for g, hd in enumerate(heads):
                s = s_all[g * BLOCK:(g + 1) * BLOCK] + bias_ref[hd] + edge
                sk = sink_ref[hd]
                m = jnp.maximum(jnp.max(s, axis=-1, keepdims=True), sk)
                p = jnp.exp(s - m)
                ls.append(jnp.sum(p, axis=-1, keepdims=True) + jnp.exp(sk - m))
                ps.append(p.astype(BF16))
            o_all = jnp.dot(jnp.concatenate(ps, axis=0), v, preferred_element_type=F32)
            for g, hd in enumerate(heads):
                o = o_all[g * BLOCK:(g + 1) * BLOCK] * (1.0 / ls[g])
                o_ref[pl.ds(r0, BLOCK), hd * HEAD_DIM:(hd + 1) * HEAD_DIM] = o
        return carry

    lax.fori_loop(0, blocks_per_tile, block_body, 0)

    og = (o_ref[...] * sg_ref[0].astype(F32)).astype(BF16)
    out_ref[0] = x_ref[0] + jnp.dot(og, wo_ref[...], preferred_element_type=F32)


def _attention(sink, q, kv, sg, x, bias, wo):
    b, s, _ = x.shape
    tq = ROW_TILE
    r = tq // BLOCK
    last_block = s // BLOCK - 1
    tile3 = lambda bi, i: (bi, i, 0)
    return pl.pallas_call(
        _attn_kernel,
        grid=(b, s // tq),
        in_specs=[
            pl.BlockSpec(memory_space=pltpu.SMEM),
            pl.BlockSpec((1, tq, D_MODEL), tile3),
            pl.BlockSpec((1, tq, D_MODEL), tile3),
            pl.BlockSpec((1, tq, D_MODEL), tile3),
            pl.BlockSpec((1, BLOCK, 2 * KV_WIDTH), lambda bi, i: (bi, jnp.maximum(i * r - 1, 0), 0)),
            pl.BlockSpec((1, tq, 2 * KV_WIDTH), tile3),
            pl.BlockSpec((1, BLOCK, 2 * KV_WIDTH), lambda bi, i: (bi, jnp.minimum((i + 1) * r, last_block), 0)),
            pl.BlockSpec((N_HEADS, BLOCK, 3 * BLOCK), lambda bi, i: (0, 0, 0)),
            pl.BlockSpec((D_MODEL, D_MODEL), lambda bi, i: (0, 0)),
        ],
        out_specs=pl.BlockSpec((1, tq, D_MODEL), tile3),
        out_shape=jax.ShapeDtypeStruct((b, s, D_MODEL), F32),
        scratch_shapes=[
            pltpu.VMEM((tq + 2 * BLOCK, 2 * KV_WIDTH), BF16),
            pltpu.VMEM((tq, D_MODEL), F32),
        ],
        compiler_params=_params("parallel", "parallel"),
        name="attention",
    )(sink, q, sg, x, kv, kv, kv, bias, wo)


def _fprep_kernel(x_ref, g_ref, wg_ref, cs_ref, ab_ref, sg_ref):
    h = _rms_scale(x_ref[0], g_ref[...]).astype(BF16)
    gate = jnp.dot(h, wg_ref[...], preferred_element_type=F32)
    sg_ref[0] = _silu(gate).astype(BF16)
    for grp in range(N_FGROUPS):
        lo, hi = grp * FGROUP_CH, (grp + 1) * FGROUP_CH
        t = jnp.dot(h[:, lo:hi], cs_ref[...], preferred_element_type=F32)
        ab_ref[0, 0, :, lo:hi] = t[:, :FGROUP_CH].astype(BF16)
        ab_ref[0, 1, :, lo:hi] = t[:, FGROUP_CH:].astype(BF16)


def _fourier_prep(x, g, wg, cs):
    b, s, _ = x.shape
    return pl.pallas_call(
        _fprep_kernel,
        grid=(b, s // ROW_TILE),
        in_specs=[
            pl.BlockSpec((1, ROW_TILE, D_MODEL), lambda bi, i: (bi, i, 0)),
            pl.BlockSpec((1, D_MODEL), lambda bi, i: (0, 0)),
            pl.BlockSpec((D_MODEL, D_MODEL), lambda bi, i: (0, 0)),
            pl.BlockSpec((FGROUP_CH, 2 * FGROUP_CH), lambda bi, i: (0, 0)),
        ],
        out_specs=[
            pl.BlockSpec((1, 2, ROW_TILE, D_MODEL), lambda bi, i: (bi, 0, i, 0)),
            pl.BlockSpec((1, ROW_TILE, D_MODEL), lambda bi, i: (bi, i, 0)),
        ],
        out_shape=[
            jax.ShapeDtypeStruct((b, 2, s, D_MODEL), BF16),
            jax.ShapeDtypeStruct((b, s, D_MODEL), BF16),
        ],
        compiler_params=_params("parallel", "parallel"),
        name="fourier_prep",
    )(x, g, wg, cs)


def _stage1_kernel(w_ref, u_ref, y_ref):
    cols = u_ref.shape[-1]
    u = u_ref[0].reshape(2 * DFT_N1, cols)
    y = jnp.dot(w_ref[...], u, preferred_element_type=F32)
    y_ref[0] = y.astype(BF16).reshape(2, DFT_N1, cols)


def _dft_stage1(w1, ab):
    b, _, _, width = ab.shape
    blk = pl.BlockSpec((1, 2, DFT_N1, STAGE1_COLS), lambda bi, i: (bi, 0, 0, i))
    return pl.pallas_call(
        _stage1_kernel,
        grid=(b, width // STAGE1_COLS),
        in_specs=[pl.BlockSpec((2 * DFT_N1, 2 * DFT_N1), lambda bi, i: (0, 0)), blk],
        out_specs=blk,
        out_shape=jax.ShapeDtypeStruct(ab.shape, BF16),
        compiler_params=_params("parallel", "parallel"),
        name="dft_stage1",
    )(w1, ab)


def _stage2_kernel(y_ref, m_ref, sg_ref, x_ref, wo_ref, g_ref, out_ref, fg_ref):
    tk1, n2 = m_ref.shape[0], m_ref.shape[1]
    for j in range(tk1):
        ycat = jnp.concatenate([y_ref[0, 0, j], y_ref[0, 1, j]], axis=0)
        f = jnp.dot(m_ref[j], ycat, preferred_element_type=F32)
        sg = sg_ref[0, :, j * D_MODEL:(j + 1) * D_MODEL].astype(F32)
        fg_ref[j * n2:(j + 1) * n2, :] = (f * sg).astype(BF16)
    z = jnp.dot(fg_ref[...], wo_ref[...], preferred_element_type=F32)
    for j in range(tk1):
        xr = x_ref[0, :, j * D_MODEL:(j + 1) * D_MODEL] + z[j * n2:(j + 1) * n2]
        out_ref[0, :, j * D_MODEL:(j + 1) * D_MODEL] = _rms_scale(xr, g_ref[...])


def _dft_stage2(y, mtab, sg, x, wo, g):
    b, _, _, n2, _ = y.shape
    tk1 = ROW_TILE // n2
    strided = pl.BlockSpec((1, n2, tk1 * D_MODEL), lambda bi, i: (bi, 0, i))
    return pl.pallas_call(
        _stage2_kernel,
        grid=(b, DFT_N1 // tk1),
        in_specs=[
            pl.BlockSpec((1, 2, tk1, n2, D_MODEL), lambda bi, i: (bi, 0, i, 0, 0)),
            pl.BlockSpec((tk1, n2, 2 * n2), lambda bi, i: (i, 0, 0)),
            strided,
            strided,
            pl.BlockSpec((D_MODEL, D_MODEL), lambda bi, i: (0, 0)),
            pl.BlockSpec((1, D_MODEL), lambda bi, i: (0, 0)),
        ],
        out_specs=strided,
        out_shape=jax.ShapeDtypeStruct(x.shape, F32),
        scratch_shapes=[pltpu.VMEM((tk1 * n2, D_MODEL), BF16)],
        compiler_params=_params("parallel", "parallel"),
        name="dft_stage2",
    )(y, mtab, sg, x, wo, g)


def _trunk(x, bias, attn_norm, wq, wkv, wgate_a, wo_a, sink, fourier_norm, wgate_f, wo_f, final_norm, cs):
    b, s, d = x.shape
    n2 = s // DFT_N1
    q, kv, sg = _inproj(x.reshape(b * s, d), attn_norm, wq, wkv, wgate_a)
    x1 = _attention(sink, q.reshape(b, s, d), kv.reshape(b, s, 2 * KV_WIDTH), sg.reshape(b, s, d), x, bias, wo_a)
    ab, sgf = _fourier_prep(x1, fourier_norm, wgate_f, cs)
    w1 = jnp.asarray(_stage1_table(s)).astype(BF16)
    mtab = jnp.asarray(_stage2_table(s)).astype(BF16)
    y = _dft_stage1(w1, ab.reshape(b, 2, DFT_N1, n2 * d))
    out = _dft_stage2(y.reshape(b, 2, DFT_N1, n2, d), mtab, sgf.reshape(b, n2, DFT_N1 * d),
                      x1.reshape(b, n2, DFT_N1 * d), wo_f, final_norm)
    return out.reshape(b, s, d)


def kernel(x_prompt, x_sample, rel_bias, attn_norm, attn_w_in, attn_w_out, attn_sink,
           fourier_norm, fourier_w_gate, fourier_w_out, final_norm):
    bucket, band_mask = _band_tables()
    bias = jnp.transpose(rel_bias.astype(F32)[jnp.asarray(bucket)], (2, 0, 1)) + jnp.asarray(band_mask)[None]
    w_in = attn_w_in[0]
    wq = w_in[:, :D_MODEL].astype(BF16)
    wkv = w_in[:, D_MODEL:D_MODEL + 2 * KV_WIDTH].astype(BF16)
    wgate_a = w_in[:, D_MODEL + 2 * KV_WIDTH:].astype(BF16)
    cs = jnp.asarray(_channel_dft_table()).astype(BF16)
    args = (bias, attn_norm[0][None], wq, wkv, wgate_a, attn_w_out[0].astype(BF16), attn_sink[0],
            fourier_norm[0][None], fourier_w_gate[0].astype(BF16), fourier_w_out[0].astype(BF16),
            final_norm[None], cs)
    return (_trunk(x_prompt, *args), _trunk(x_sample, *args))
```

```python
import functools
import math

import jax
import jax.numpy as jnp
import numpy as np
from jax import lax
from jax.experimental import pallas as pl
from jax.experimental.pallas import tpu as pltpu

D_MODEL = 1024
HEAD_DIM = 64
N_HEADS = 16
N_KV_HEADS = 4
GROUP = 4
KV_WIDTH = N_KV_HEADS * HEAD_DIM
WINDOW = 128
BLOCK = 128
NUM_BUCKETS = 32
MAX_DISTANCE = 128
N_FGROUPS = 4
FGROUP_CH = 256
RMS_EPS = 1e-6
MASK_VALUE = -1e30

DFT_N1 = 128
ROW_TILE = 512
BF16_SUBLANES = 16
F32_SUBLANES = 8
STAGE1_CH = 512
STAGE2_ROWS = 1024
VMEM_LIMIT = 48 * 1024 * 1024

BF16 = jnp.bfloat16
F32 = jnp.float32


def _params(*sem):
    return pltpu.CompilerParams(dimension_semantics=sem, vmem_limit_bytes=VMEM_LIMIT)


def _rms_scale(x, g):
    inv = lax.rsqrt(jnp.mean(x * x, axis=-1, keepdims=True) + RMS_EPS)
    return x * inv * g


def _silu(z):
    return z * (1.0 / (1.0 + jnp.exp(-z)))


def _t5_bucket_np(rel):
    half = NUM_BUCKETS // 2
    n = -rel
    ret = (n < 0).astype(np.int32) * half
    n = np.abs(n)
    max_exact = half // 2
    is_small = n < max_exact
    large = max_exact + (np.log(np.maximum(n, 1) / max_exact) / math.log(MAX_DISTANCE / max_exact)
                         * (half - max_exact)).astype(np.int32)
    large = np.minimum(large, half - 1)
    return (ret + np.where(is_small, n, large)).astype(np.int32)


@functools.lru_cache(maxsize=None)
def _band_tables():
    qi = np.arange(BLOCK)[:, None]
    kj = np.arange(3 * BLOCK)[None, :]
    rel = kj - BLOCK - qi
    band = np.abs(rel) <= WINDOW
    return _t5_bucket_np(rel), np.where(band, 0.0, MASK_VALUE).astype(np.float32)


@functools.lru_cache(maxsize=None)
def _channel_dft_table():
    c = np.arange(FGROUP_CH)
    ang = 2.0 * np.pi * ((c[:, None] * c[None, :]) % FGROUP_CH) / FGROUP_CH
    return (np.concatenate([np.cos(ang), np.sin(ang)], axis=1) / 16.0).astype(np.float32)


@functools.lru_cache(maxsize=None)
def _stage1_table(seq):
    k = np.arange(DFT_N1)
    ang = 2.0 * np.pi * ((k[:, None] * k[None, :]) % DFT_N1) / DFT_N1
    c, s = np.cos(ang), np.sin(ang)
    w = np.block([[c, -s], [-s, -c]]) / math.sqrt(seq)
    return w.astype(np.float32)


@functools.lru_cache(maxsize=None)
def _stage2_table(seq):
    n2_len = seq // DFT_N1
    k1 = np.arange(DFT_N1)[:, None, None]
    k2 = np.arange(n2_len)[None, :, None]
    n2 = np.arange(n2_len)[None, None, :]
    ang = 2.0 * np.pi * (((k1 + DFT_N1 * k2) * n2) % seq) / seq
    return np.concatenate([np.cos(ang), np.sin(ang)], axis=2).astype(np.float32)


def _inproj_kernel(x_ref, g_ref, wq_ref, wkv_ref, wg_ref, q_ref, kv_ref, sg_ref):
    h = _rms_scale(x_ref[...], g_ref[...]).astype(BF16)
    q = jnp.dot(h, wq_ref[...], preferred_element_type=F32)
    q_ref[...] = (q * (HEAD_DIM ** -0.5)).astype(BF16)
    kv_ref[...] = jnp.dot(h, wkv_ref[...], preferred_element_type=F32).astype(BF16)
    gate = jnp.dot(h, wg_ref[...], preferred_element_type=F32)
    sg_ref[...] = _silu(gate).astype(BF16)


def _inproj(x2, g, wq, wkv, wg):
    t = x2.shape[0]
    row = lambda i: (i, 0)
    const = lambda i: (0, 0)
    return pl.pallas_call(
        _inproj_kernel,
        grid=(t // ROW_TILE,),
        in_specs=[
            pl.BlockSpec((ROW_TILE, D_MODEL), row),
            pl.BlockSpec((1, D_MODEL), const),
            pl.BlockSpec((D_MODEL, D_MODEL), const),
            pl.BlockSpec((D_MODEL, 2 * KV_WIDTH), const),
            pl.BlockSpec((D_MODEL, D_MODEL), const),
        ],
        out_specs=[
            pl.BlockSpec((ROW_TILE, D_MODEL), row),
            pl.BlockSpec((ROW_TILE, 2 * KV_WIDTH), row),
            pl.BlockSpec((ROW_TILE, D_MODEL), row),
        ],
        out_shape=[
            jax.ShapeDtypeStruct((t, D_MODEL), BF16),
            jax.ShapeDtypeStruct((t, 2 * KV_WIDTH), BF16),
            jax.ShapeDtypeStruct((t, D_MODEL), BF16),
        ],
        compiler_params=_params("parallel"),
        name="inproj",
    )(x2, g, wq, wkv, wg)


def _attn_kernel(sink_ref, q_ref, sg_ref, x_ref, kvp_ref, kvc_ref, kvn_ref, bias_ref, wo_ref,
                 out_ref, kvw_ref, o_ref):
    tq = q_ref.shape[1]
    blocks_per_tile = tq // BLOCK
    n_blocks = pl.num_programs(1) * blocks_per_tile
    tile = pl.program_id(1)

    kvw_ref[0:BLOCK, :] = kvp_ref[0]
    kvw_ref[BLOCK:BLOCK + tq, :] = kvc_ref[0]
    kvw_ref[BLOCK + tq:, :] = kvn_ref[0]

    col = lax.broadcasted_iota(jnp.int32, (1, 3 * BLOCK), 1)

    def block_body(j, carry):
        gb = tile * blocks_per_tile + j
        outside = jnp.logical_or(jnp.logical_and(gb == 0, col < BLOCK),
                                 jnp.logical_and(gb == n_blocks - 1, col >= 2 * BLOCK))
        edge = jnp.where(outside, MASK_VALUE, 0.0)
        r0 = pl.multiple_of(j * BLOCK, BLOCK)
        for hh in range(N_KV_HEADS):
            k = kvw_ref[pl.ds(r0, 3 * BLOCK), hh * HEAD_DIM:(hh + 1) * HEAD_DIM]
            v = kvw_ref[pl.ds(r0, 3 * BLOCK), KV_WIDTH + hh * HEAD_DIM:KV_WIDTH + (hh + 1) * HEAD_DIM]
            heads = [hh * GROUP + g for g in range(GROUP)]
            qs = jnp.concatenate(
                [q_ref[0, pl.ds(r0, BLOCK), hd * HEAD_DIM:(hd + 1) * HEAD_DIM] for hd in heads], axis=0)
            s_all = lax.dot_general(qs, k, (((1,), (1,)), ((), ())), preferred_element_type=F32)
            ps, ls = [], []
            for g, hd in enumerate(heads):
                s = s_all[g * BLOCK:(g + 1) * BLOCK] + bias_ref[hd] + edge
                sk = sink_ref[hd]
                m = jnp.maximum(jnp.max(s, axis=-1, keepdims=True), sk)
                p = jnp.exp(s - m)
                ls.append(jnp.sum(p, axis=-1, keepdims=True) + jnp.exp(sk - m))
                ps.append(p.astype(BF16))
            o_all = jnp.dot(jnp.concatenate(ps, axis=0), v, preferred_element_type=F32)
            for g, hd in enumerate(heads):
                o = o_all[g * BLOCK:(g + 1) * BLOCK] * (1.0 / ls[g])
                o_ref[pl.ds(r0, BLOCK), hd * HEAD_DIM:(hd + 1) * HEAD_DIM] = o
        return carry

    lax.fori_loop(0, blocks_per_tile, block_body, 0)

    og = (o_ref[...] * sg_ref[0].astype(F32)).astype(BF16)
    out_ref[0] = x_ref[0] + jnp.dot(og, wo_ref[...], preferred_element_type=F32)


def _attention(sink, q, kv, sg, x, bias, wo):
    b, s, _ = x.shape
    tq = ROW_TILE
    r = tq // BLOCK
    last_block = s // BLOCK - 1
    tile3 = lambda bi, i: (bi, i, 0)
    return pl.pallas_call(
        _attn_kernel,
        grid=(b, s // tq),
        in_specs=[
            pl.BlockSpec(memory_space=pltpu.SMEM),
            pl.BlockSpec((1, tq, D_MODEL), tile3),
            pl.BlockSpec((1, tq, D_MODEL), tile3),
            pl.BlockSpec((1, tq, D_MODEL), tile3),
            pl.BlockSpec((1, BLOCK, 2 * KV_WIDTH), lambda bi, i: (bi, jnp.maximum(i * r - 1, 0), 0)),
            pl.BlockSpec((1, tq, 2 * KV_WIDTH), tile3),
            pl.BlockSpec((1, BLOCK, 2 * KV_WIDTH), lambda bi, i: (bi, jnp.minimum((i + 1) * r, last_block), 0)),
            pl.BlockSpec((N_HEADS, BLOCK, 3 * BLOCK), lambda bi, i: (0, 0, 0)),
            pl.BlockSpec((D_MODEL, D_MODEL), lambda bi, i: (0, 0)),
        ],
        out_specs=pl.BlockSpec((1, tq, D_MODEL), tile3),
        out_shape=jax.ShapeDtypeStruct((b, s, D_MODEL), F32),
        scratch_shapes=[
            pltpu.VMEM((tq + 2 * BLOCK, 2 * KV_WIDTH), BF16),
            pltpu.VMEM((tq, D_MODEL), F32),
        ],
        compiler_params=_params("parallel", "parallel"),
        name="attention",
    )(sink, q, sg, x, kv, kv, kv, bias, wo)


def _chdft_kernel(x_ref, g_ref, cs_ref, ab_ref):
    h = _rms_scale(x_ref[0], g_ref[...]).astype(BF16)
    for grp in range(N_FGROUPS):
        lo, hi = grp * FGROUP_CH, (grp + 1) * FGROUP_CH
        t = jnp.dot(h[:, lo:hi], cs_ref[...], preferred_element_type=F32)
        ab_ref[0, 0, :, lo:hi] = t[:, :FGROUP_CH].astype(BF16)
        ab_ref[0, 1, :, lo:hi] = t[:, FGROUP_CH:].astype(BF16)


def _channel_dft(x, g, cs):
    b, s, _ = x.shape
    return pl.pallas_call(
        _chdft_kernel,
        grid=(b, s // ROW_TILE),
        in_specs=[
            pl.BlockSpec((1, ROW_TILE, D_MODEL), lambda bi, i: (bi, i, 0)),
            pl.BlockSpec((1, D_MODEL), lambda bi, i: (0, 0)),
            pl.BlockSpec((FGROUP_CH, 2 * FGROUP_CH), lambda bi, i: (0, 0)),
        ],
        out_specs=pl.BlockSpec((1, 2, ROW_TILE, D_MODEL), lambda bi, i: (bi, 0, i, 0)),
        out_shape=jax.ShapeDtypeStruct((b, 2, s, D_MODEL), BF16),
        compiler_params=_params("parallel", "parallel"),
        name="channel_dft",
    )(x, g, cs)


def _stage1_kernel(w_ref, u_ref, y_ref):
    _, _, n1, tn2, ch = u_ref.shape
    u = jnp.swapaxes(u_ref[0].reshape(2 * n1, tn2, ch), 0, 1)
    ys = [jnp.dot(w_ref[...], u[j], preferred_element_type=F32).astype(BF16) for j in range(tn2)]
    y = jnp.swapaxes(jnp.stack(ys, axis=0), 0, 1)
    y_ref[0] = y.reshape(2, n1, tn2, ch)


def _dft_stage1(w1, ab5):
    b, _, n1, n2, d = ab5.shape
    blk = pl.BlockSpec((1, 2, n1, BF16_SUBLANES, STAGE1_CH), lambda bi, i, c: (bi, 0, 0, i, c))
    return pl.pallas_call(
        _stage1_kernel,
        grid=(b, n2 // BF16_SUBLANES, d // STAGE1_CH),
        in_specs=[pl.BlockSpec((2 * n1, 2 * n1), lambda bi, i, c: (0, 0)), blk],
        out_specs=blk,
        out_shape=jax.ShapeDtypeStruct(ab5.shape, BF16),
        compiler_params=_params("parallel", "parallel", "parallel"),
        name="dft_stage1",
    )(w1, ab5)


def _stage2_kernel(y_ref, m_ref, x_ref, gf_ref, wg_ref, wo_ref, gl_ref, out_ref, fg_ref):
    tk1, n2 = m_ref.shape[0], m_ref.shape[1]
    x = jnp.swapaxes(x_ref[0], 0, 1).reshape(tk1 * n2, D_MODEL)
    h = _rms_scale(x, gf_ref[...]).astype(BF16)
    sg = _silu(jnp.dot(h, wg_ref[...], preferred_element_type=F32))
    for j in range(tk1):
        ycat = jnp.concatenate([y_ref[0, 0, j], y_ref[0, 1, j]], axis=0)
        f = jnp.dot(m_ref[j], ycat, preferred_element_type=F32)
        fg_ref[j * n2:(j + 1) * n2, :] = (f * sg[j * n2:(j + 1) * n2]).astype(BF16)
    xr = x + jnp.dot(fg_ref[...], wo_ref[...], preferred_element_type=F32)
    o = _rms_scale(xr, gl_ref[...]).reshape(tk1, n2, D_MODEL)
    out_ref[0] = jnp.swapaxes(o, 0, 1)


def _dft_stage2(y5, mtab, x4, gf, wg, wo, gl):
    b, _, n1, n2, d = y5.shape
    tk1 = STAGE2_ROWS // n2
    rows = pl.BlockSpec((1, n2, tk1, d), lambda bi, i: (bi, 0, i, 0))
    const = lambda bi, i: (0, 0)
    return pl.pallas_call(
        _stage2_kernel,
        grid=(b, n1 // tk1),
        in_specs=[
            pl.BlockSpec((1, 2, tk1, n2, d), lambda bi, i: (bi, 0, i, 0, 0)),
            pl.BlockSpec((tk1, n2, 2 * n2), lambda bi, i: (i, 0, 0)),
            rows,
            pl.BlockSpec((1, d), const),
            pl.BlockSpec((d, d), const),
            pl.BlockSpec((d, d), const),
            pl.BlockSpec((1, d), const),
        ],
        out_specs=rows,
        out_shape=jax.ShapeDtypeStruct(x4.shape, F32),
        scratch_shapes=[pltpu.VMEM((tk1 * n2, d), BF16)],
        compiler_params=_params("parallel", "parallel"),
        name="dft_stage2",
    )(y5, mtab, x4, gf, wg, wo, gl)


def _trunk(x, bias, attn_norm, wq, wkv, wgate_a, wo_a, sink, fourier_norm, wgate_f, wo_f, final_norm, cs):
    b, s, d = x.shape
    n2 = s // DFT_N1
    q, kv, sg = _inproj(x.reshape(b * s, d), attn_norm, wq, wkv, wgate_a)
    x1 = _attention(sink, q.reshape(b, s, d), kv.reshape(b, s, 2 * KV_WIDTH), sg.reshape(b, s, d), x, bias, wo_a)
    ab = _channel_dft(x1, fourier_norm, cs)
    w1 = jnp.asarray(_stage1_table(s)).astype(BF16)
    mtab = jnp.asarray(_stage2_table(s)).astype(BF16)
    y5 = _dft_stage1(w1, ab.reshape(b, 2, DFT_N1, n2, d))
    out = _dft_stage2(y5, mtab, x1.reshape(b, n2, DFT_N1, d), fourier_norm, wgate_f, wo_f, final_norm)
    return out.reshape(b, s, d)


def kernel(x_prompt, x_sample, rel_bias, attn_norm, attn_w_in, attn_w_out, attn_sink,
           fourier_norm, fourier_w_gate, fourier_w_out, final_norm):
    bucket, band_mask = _band_tables()
    onehot = (jnp.asarray(bucket.reshape(-1))[:, None] == jnp.arange(NUM_BUCKETS)[None, :]).astype(F32)
    bias = jnp.einsum("pb,bh->hp", onehot, rel_bias.astype(F32), precision=lax.Precision.HIGHEST)
    bias = bias.reshape(N_HEADS, BLOCK, 3 * BLOCK) + jnp.asarray(band_mask)[None]
    w_in = attn_w_in[0]
    wq = w_in[:, :D_MODEL].astype(BF16)
    wkv = w_in[:, D_MODEL:D_MODEL + 2 * KV_WIDTH].astype(BF16)
    wgate_a = w_in[:, D_MODEL + 2 * KV_WIDTH:].astype(BF16)
    cs = jnp.asarray(_channel_dft_table()).astype(BF16)
    args = (bias, attn_norm[0][None], wq, wkv, wgate_a, attn_w_out[0].astype(BF16), attn_sink[0],
            fourier_norm[0][None], fourier_w_gate[0].astype(BF16), fourier_w_out[0].astype(BF16),
            final_norm[None], cs)
    return (_trunk(x_prompt, *args), _trunk(x_sample, *args))
```

```python
import functools
import math

import jax
import jax.numpy as jnp
import numpy as np
from jax import lax
from jax.experimental import pallas as pl
from jax.experimental.pallas import tpu as pltpu

D_MODEL = 1024
HEAD_DIM = 64
N_HEADS = 16
N_KV_HEADS = 4
GROUP = 4
KV_WIDTH = N_KV_HEADS * HEAD_DIM
WINDOW = 128
BLOCK = 128
NUM_BUCKETS = 32
MAX_DISTANCE = 128
N_FGROUPS = 4
FGROUP_CH = 256
RMS_EPS = 1e-6
MASK_VALUE = -1e30
LOG2E = math.log2(math.e)

DFT_N1 = 128
ROW_TILE = 512
BF16_SUBLANES = 16
F32_SUBLANES = 8
STAGE1_CH = 512
STAGE2_ROWS = 1024
VMEM_LIMIT = 48 * 1024 * 1024

BF16 = jnp.bfloat16
F32 = jnp.float32


def _params(*sem):
    return pltpu.CompilerParams(dimension_semantics=sem, vmem_limit_bytes=VMEM_LIMIT)


def _rms_scale(x, g):
    inv = lax.rsqrt(jnp.mean(x * x, axis=-1, keepdims=True) + RMS_EPS)
    return x * inv * g


def _silu(z):
    return z * (1.0 / (1.0 + jnp.exp(-z)))


def _t5_bucket_np(rel):
    half = NUM_BUCKETS // 2
    n = -rel
    ret = (n < 0).astype(np.int32) * half
    n = np.abs(n)
    max_exact = half // 2
    is_small = n < max_exact
    large = max_exact + (np.log(np.maximum(n, 1) / max_exact) / math.log(MAX_DISTANCE / max_exact)
                         * (half - max_exact)).astype(np.int32)
    large = np.minimum(large, half - 1)
    return (ret + np.where(is_small, n, large)).astype(np.int32)


@functools.lru_cache(maxsize=None)
def _band_tables():
    qi = np.arange(BLOCK)[:, None]
    kj = np.arange(3 * BLOCK)[None, :]
    rel = kj - BLOCK - qi
    band = np.abs(rel) <= WINDOW
    return _t5_bucket_np(rel), np.where(band, 0.0, MASK_VALUE).astype(np.float32)


@functools.lru_cache(maxsize=None)
def _channel_dft_table():
    c = np.arange(FGROUP_CH)
    ang = 2.0 * np.pi * ((c[:, None] * c[None, :]) % FGROUP_CH) / FGROUP_CH
    return (np.concatenate([np.cos(ang), np.sin(ang)], axis=1) / 16.0).astype(np.float32)


@functools.lru_cache(maxsize=None)
def _stage1_table(seq):
    k = np.arange(DFT_N1)
    ang = 2.0 * np.pi * ((k[:, None] * k[None, :]) % DFT_N1) / DFT_N1
    c, s = np.cos(ang), np.sin(ang)
    w = np.block([[c, -s], [-s, -c]]) / math.sqrt(seq)
    return w.astype(np.float32)


@functools.lru_cache(maxsize=None)
def _stage2_table(seq):
    n2_len = seq // DFT_N1
    k1 = np.arange(DFT_N1)[:, None, None]
    k2 = np.arange(n2_len)[None, :, None]
    n2 = np.arange(n2_len)[None, None, :]
    ang = 2.0 * np.pi * (((k1 + DFT_N1 * k2) * n2) % seq) / seq
    return np.concatenate([np.cos(ang), np.sin(ang)], axis=2).astype(np.float32)


def _inproj_kernel(x_ref, g_ref, wq_ref, wkv_ref, wg_ref, q_ref, kv_ref, sg_ref):
    h = _rms_scale(x_ref[...], g_ref[...]).astype(BF16)
    q = jnp.dot(h, wq_ref[...], preferred_element_type=F32)
    q_ref[...] = (q * (HEAD_DIM ** -0.5 * LOG2E)).astype(BF16)
    kv_ref[...] = jnp.dot(h, wkv_ref[...], preferred_element_type=F32).astype(BF16)
    gate = jnp.dot(h, wg_ref[...], preferred_element_type=F32)
    sg_ref[...] = _silu(gate).astype(BF16)


def _inproj(x2, g, wq, wkv, wg):
    t = x2.shape[0]
    row = lambda i: (i, 0)
    const = lambda i: (0, 0)
    return pl.pallas_call(
        _inproj_kernel,
        grid=(t // ROW_TILE,),
        in_specs=[
            pl.BlockSpec((ROW_TILE, D_MODEL), row),
            pl.BlockSpec((1, D_MODEL), const),
            pl.BlockSpec((D_MODEL, D_MODEL), const),
            pl.BlockSpec((D_MODEL, 2 * KV_WIDTH), const),
            pl.BlockSpec((D_MODEL, D_MODEL), const),
        ],
        out_specs=[
            pl.BlockSpec((ROW_TILE, D_MODEL), row),
            pl.BlockSpec((ROW_TILE, 2 * KV_WIDTH), row),
            pl.BlockSpec((ROW_TILE, D_MODEL), row),
        ],
        out_shape=[
            jax.ShapeDtypeStruct((t, D_MODEL), BF16),
            jax.ShapeDtypeStruct((t, 2 * KV_WIDTH), BF16),
            jax.ShapeDtypeStruct((t, D_MODEL), BF16),
        ],
        compiler_params=_params("parallel"),
        name="inproj",
    )(x2, g, wq, wkv, wg)


def _attn_kernel(sink_ref, q_ref, sg_ref, x_ref, kvp_ref, kvc_ref, kvn_ref, bias_ref, wo_ref,
                 out_ref, kvw_ref, kz_ref, vd_ref, s_ref, o_ref):
    tq = q_ref.shape[1]
    blocks_per_tile = tq // BLOCK
    n_blocks = pl.num_programs(1) * blocks_per_tile
    tile = pl.program_id(1)

    kvw_ref[0:BLOCK, :] = kvp_ref[0]
    kvw_ref[BLOCK:BLOCK + tq, :] = kvc_ref[0]
    kvw_ref[BLOCK + tq:, :] = kvn_ref[0]

    low = lax.broadcasted_iota(jnp.int32, (1, 2 * HEAD_DIM), 1) < HEAD_DIM
    for c in range(N_KV_HEADS // 2):
        kc = kvw_ref[:, 2 * c * HEAD_DIM:(2 * c + 2) * HEAD_DIM]
        vc = kvw_ref[:, KV_WIDTH + 2 * c * HEAD_DIM:KV_WIDTH + (2 * c + 2) * HEAD_DIM]
        kc_sw = jnp.concatenate([kc[:, HEAD_DIM:], kc[:, :HEAD_DIM]], axis=1)
        vc_sw = jnp.concatenate([vc[:, HEAD_DIM:], vc[:, :HEAD_DIM]], axis=1)
        zero = jnp.zeros_like(kc)
        for t, tab in enumerate((jnp.where(low, kc, zero), jnp.where(low, zero, kc_sw),
                                 jnp.where(low, kc_sw, zero), jnp.where(low, zero, kc))):
            kz_ref[:, (4 * c + t) * 2 * HEAD_DIM:(4 * c + t + 1) * 2 * HEAD_DIM] = tab
        vd_ref[:, (2 * c) * 2 * HEAD_DIM:(2 * c + 1) * 2 * HEAD_DIM] = jnp.where(low, vc, vc_sw)
        vd_ref[:, (2 * c + 1) * 2 * HEAD_DIM:(2 * c + 2) * 2 * HEAD_DIM] = jnp.where(low, vc_sw, vc)

    nt_dims = (((1,), (1,)), ((), ()))

    def scores(j, slot):
        r0 = j * BLOCK
        for hh in range(N_KV_HEADS):
            qp = q_ref[0, pl.ds(r0, BLOCK), hh * 4 * HEAD_DIM:(hh + 1) * 4 * HEAD_DIM]
            lhs = jnp.concatenate([qp[:, :2 * HEAD_DIM], qp[:, 2 * HEAD_DIM:]], axis=0)
            for par in range(2):
                kz = kz_ref[pl.ds(r0, 3 * BLOCK), (2 * hh + par) * 2 * HEAD_DIM:(2 * hh + par + 1) * 2 * HEAD_DIM]
                s2 = lax.dot_general(lhs, kz, nt_dims, preferred_element_type=F32)
                s_ref[slot, hh * GROUP + par] = s2[:BLOCK]
                s_ref[slot, hh * GROUP + 2 + par] = s2[BLOCK:]

    def finish(j, slot):
        r0 = j * BLOCK
        gb = tile * blocks_per_tile + j
        variant = jnp.where(gb == 0, 0, jnp.where(gb == n_blocks - 1, 2, 1))
        for hh in range(N_KV_HEADS):
            ps, inv_ls = [], []
            for g in range(GROUP):
                hd = hh * GROUP + g
                s = s_ref[slot, hd] + bias_ref[variant, hd]
                sk = sink_ref[hd]
                m = jnp.maximum(jnp.max(s, axis=-1, keepdims=True), sk)
                p = jnp.exp2(s - m)
                inv_ls.append(1.0 / (jnp.sum(p, axis=-1, keepdims=True) + jnp.exp2(sk - m)))
                ps.append(p.astype(BF16))
            vd = vd_ref[pl.ds(r0, 3 * BLOCK), hh * 2 * HEAD_DIM:(hh + 1) * 2 * HEAD_DIM]
            o2 = jnp.dot(jnp.concatenate(ps, axis=0), vd, preferred_element_type=F32)
            for t in range(2):
                even = o2[2 * t * BLOCK:(2 * t + 1) * BLOCK] * inv_ls[2 * t]
                odd = o2[(2 * t + 1) * BLOCK:(2 * t + 2) * BLOCK] * inv_ls[2 * t + 1]
                pair = 2 * hh + t
                o_ref[pl.ds(r0, BLOCK), pair * 2 * HEAD_DIM:(pair + 1) * 2 * HEAD_DIM] = jnp.where(low, even, odd)

    scores(0, 0)
    for j in range(blocks_per_tile):
        if j + 1 < blocks_per_tile:
            scores(j + 1, (j + 1) % 2)
        finish(j, j % 2)

    og = (o_ref[...] * sg_ref[0].astype(F32)).astype(BF16)
    out_ref[0] = x_ref[0] + jnp.dot(og, wo_ref[...], preferred_element_type=F32)


def _attention(sink, q, kv, sg, x, bias, wo):
    b, s, _ = x.shape
    tq = ROW_TILE
    r = tq // BLOCK
    last_block = s // BLOCK - 1
    tile3 = lambda bi, i: (bi, i, 0)
    return pl.pallas_call(
        _attn_kernel,
        grid=(b, s // tq),
        in_specs=[
            pl.BlockSpec(memory_space=pltpu.SMEM),
            pl.BlockSpec((1, tq, D_MODEL), tile3),
            pl.BlockSpec((1, tq, D_MODEL), tile3),
            pl.BlockSpec((1, tq, D_MODEL), tile3),
            pl.BlockSpec((1, BLOCK, 2 * KV_WIDTH), lambda bi, i: (bi, jnp.maximum(i * r - 1, 0), 0)),
            pl.BlockSpec((1, tq, 2 * KV_WIDTH), tile3),
            pl.BlockSpec((1, BLOCK, 2 * KV_WIDTH), lambda bi, i: (bi, jnp.minimum((i + 1) * r, last_block), 0)),
            pl.BlockSpec((3, N_HEADS, BLOCK, 3 * BLOCK), lambda bi, i: (0, 0, 0, 0),
                         pipeline_mode=pl.Buffered(1)),
            pl.BlockSpec((D_MODEL, D_MODEL), lambda bi, i: (0, 0), pipeline_mode=pl.Buffered(1)),
        ],
        out_specs=pl.BlockSpec((1, tq, D_MODEL), tile3),
        out_shape=jax.ShapeDtypeStruct((b, s, D_MODEL), F32),
        scratch_shapes=[
            pltpu.VMEM((tq + 2 * BLOCK, 2 * KV_WIDTH), BF16),
            pltpu.VMEM((tq + 2 * BLOCK, 2 * N_KV_HEADS * 2 * HEAD_DIM), BF16),
            pltpu.VMEM((tq + 2 * BLOCK, N_KV_HEADS * 2 * HEAD_DIM), BF16),
            pltpu.VMEM((2, N_HEADS, BLOCK, 3 * BLOCK), F32),
            pltpu.VMEM((tq, D_MODEL), F32),
        ],
        compiler_params=_params("parallel", "parallel"),
        name="attention",
    )(sink, q, sg, x, kv, kv, kv, bias, wo)


def _chdft_kernel(x_ref, g_ref, cs_ref, ab_ref):
    h = _rms_scale(x_ref[0], g_ref[...]).astype(BF16)
    for grp in range(N_FGROUPS):
        lo, hi = grp * FGROUP_CH, (grp + 1) * FGROUP_CH
        t = jnp.dot(h[:, lo:hi], cs_ref[...], preferred_element_type=F32)
        ab_ref[0, 0, :, lo:hi] = t[:, :FGROUP_CH].astype(BF16)
        ab_ref[0, 1, :, lo:hi] = t[:, FGROUP_CH:].astype(BF16)


def _channel_dft(x, g, cs):
    b, s, _ = x.shape
    return pl.pallas_call(
        _chdft_kernel,
        grid=(b, s // ROW_TILE),
        in_specs=[
            pl.BlockSpec((1, ROW_TILE, D_MODEL), lambda bi, i: (bi, i, 0)),
            pl.BlockSpec((1, D_MODEL), lambda bi, i: (0, 0)),
            pl.BlockSpec((FGROUP_CH, 2 * FGROUP_CH), lambda bi, i: (0, 0)),
        ],
        out_specs=pl.BlockSpec((1, 2, ROW_TILE, D_MODEL), lambda bi, i: (bi, 0, i, 0)),
        out_shape=jax.ShapeDtypeStruct((b, 2, s, D_MODEL), BF16),
        compiler_params=_params("parallel", "parallel"),
        name="channel_dft",
    )(x, g, cs)


def _stage1_kernel(w_ref, u_ref, y_ref):
    _, _, n1, tn2, ch = u_ref.shape
    u = jnp.swapaxes(u_ref[0].reshape(2 * n1, tn2, ch), 0, 1)
    ys = [jnp.dot(w_ref[...], u[j], preferred_element_type=F32).astype(BF16) for j in range(tn2)]
    y = jnp.swapaxes(jnp.stack(ys, axis=0), 0, 1)
    y_ref[0] = y.reshape(2, n1, tn2, ch)


def _dft_stage1(w1, ab5):
    b, _, n1, n2, d = ab5.shape
    blk = pl.BlockSpec((1, 2, n1, BF16_SUBLANES, STAGE1_CH), lambda bi, i, c: (bi, 0, 0, i, c))
    return pl.pallas_call(
        _stage1_kernel,
        grid=(b, n2 // BF16_SUBLANES, d // STAGE1_CH),
        in_specs=[pl.BlockSpec((2 * n1, 2 * n1), lambda bi, i, c: (0, 0)), blk],
        out_specs=blk,
        out_shape=jax.ShapeDtypeStruct(ab5.shape, BF16),
        compiler_params=_params("parallel", "parallel", "parallel"),
        name="dft_stage1",
    )(w1, ab5)


def _stage2_kernel(y_ref, m_ref, x_ref, gf_ref, wg_ref, wo_ref, gl_ref, out_ref, fg_ref):
    tk1, n2 = m_ref.shape[0], m_ref.shape[1]
    x = jnp.swapaxes(x_ref[0], 0, 1).reshape(tk1 * n2, D_MODEL)
    h = _rms_scale(x, gf_ref[...]).astype(BF16)
    sg = _silu(jnp.dot(h, wg_ref[...], preferred_element_type=F32))
    for j in range(tk1):
        ycat = jnp.concatenate([y_ref[0, 0, j], y_ref[0, 1, j]], axis=0)
        f = jnp.dot(m_ref[j], ycat, preferred_element_type=F32)
        fg_ref[j * n2:(j + 1) * n2, :] = (f * sg[j * n2:(j + 1) * n2]).astype(BF16)
    xr = x + jnp.dot(fg_ref[...], wo_ref[...], preferred_element_type=F32)
    o = _rms_scale(xr, gl_ref[...]).reshape(tk1, n2, D_MODEL)
    out_ref[0] = jnp.swapaxes(o, 0, 1)


def _dft_stage2(y5, mtab, x4, gf, wg, wo, gl):
    b, _, n1, n2, d = y5.shape
    tk1 = STAGE2_ROWS // n2
    rows = pl.BlockSpec((1, n2, tk1, d), lambda bi, i: (bi, 0, i, 0))
    const = lambda bi, i: (0, 0)
    return pl.pallas_call(
        _stage2_kernel,
        grid=(b, n1 // tk1),
        in_specs=[
            pl.BlockSpec((1, 2, tk1, n2, d), lambda bi, i: (bi, 0, i, 0, 0)),
            pl.BlockSpec((tk1, n2, 2 * n2), lambda bi, i: (i, 0, 0)),
            rows,
            pl.BlockSpec((1, d), const),
            pl.BlockSpec((d, d), const),
            pl.BlockSpec((d, d), const),
            pl.BlockSpec((1, d), const),
        ],
        out_specs=rows,
        out_shape=jax.ShapeDtypeStruct(x4.shape, F32),
        scratch_shapes=[pltpu.VMEM((tk1 * n2, d), BF16)],
        compiler_params=_params("parallel", "parallel"),
        name="dft_stage2",
    )(y5, mtab, x4, gf, wg, wo, gl)


def _trunk(x, bias, attn_norm, wq, wkv, wgate_a, wo_a, sink, fourier_norm, wgate_f, wo_f, final_norm, cs):
    b, s, d = x.shape
    n2 = s // DFT_N1
    q, kv, sg = _inproj(x.reshape(b * s, d), attn_norm, wq, wkv, wgate_a)
    x1 = _attention(sink, q.reshape(b, s, d), kv.reshape(b, s, 2 * KV_WIDTH), sg.reshape(b, s, d), x, bias, wo_a)
    ab = _channel_dft(x1, fourier_norm, cs)
    w1 = jnp.asarray(_stage1_table(s)).astype(BF16)
    mtab = jnp.asarray(_stage2_table(s)).astype(BF16)
    y5 = _dft_stage1(w1, ab.reshape(b, 2, DFT_N1, n2, d))
    out = _dft_stage2(y5, mtab, x1.reshape(b, n2, DFT_N1, d), fourier_norm, wgate_f, wo_f, final_norm)
    return out.reshape(b, s, d)


def kernel(x_prompt, x_sample, rel_bias, attn_norm, attn_w_in, attn_w_out, attn_sink,
           fourier_norm, fourier_w_gate, fourier_w_out, final_norm):
    bucket, band_mask = _band_tables()
    onehot = (jnp.asarray(bucket.reshape(-1))[:, None] == jnp.arange(NUM_BUCKETS)[None, :]).astype(F32)
    bias = jnp.einsum("pb,bh->hp", onehot, rel_bias.astype(F32), precision=lax.Precision.HIGHEST)
    bias = bias.reshape(N_HEADS, BLOCK, 3 * BLOCK) * LOG2E + jnp.asarray(band_mask)[None]
    kcol = jnp.arange(3 * BLOCK)[None, None, :]
    bias = jnp.stack([jnp.where(kcol < BLOCK, MASK_VALUE, bias), bias,
                      jnp.where(kcol >= 2 * BLOCK, MASK_VALUE, bias)], axis=0)
    w_in = attn_w_in[0]
    wq = w_in[:, :D_MODEL].astype(BF16)
    wkv = w_in[:, D_MODEL:D_MODEL + 2 * KV_WIDTH].astype(BF16)
    wgate_a = w_in[:, D_MODEL + 2 * KV_WIDTH:].astype(BF16)
    cs = jnp.asarray(_channel_dft_table()).astype(BF16)
    args = (bias, attn_norm[0][None], wq, wkv, wgate_a, attn_w_out[0].astype(BF16), attn_sink[0] * LOG2E,
            fourier_norm[0][None], fourier_w_gate[0].astype(BF16), fourier_w_out[0].astype(BF16),
            final_norm[None], cs)
    return (_trunk(x_prompt, *args), _trunk(x_sample, *args))
```

```python
import functools
import math

import jax
import jax.numpy as jnp
import numpy as np
from jax import lax
from jax.experimental import pallas as pl
from jax.experimental.pallas import tpu as pltpu

D_MODEL = 1024
HEAD_DIM = 64
N_HEADS = 16
N_KV_HEADS = 4
GROUP = 4
KV_WIDTH = N_KV_HEADS * HEAD_DIM
WINDOW = 128
BLOCK = 128
NUM_BUCKETS = 32
MAX_DISTANCE = 128
N_FGROUPS = 4
FGROUP_CH = 256
RMS_EPS = 1e-6
MASK_VALUE = -1e30
LOG2E = math.log2(math.e)

DFT_N1 = 128
ROW_TILE = 512
BF16_SUBLANES = 16
F32_SUBLANES = 8
STAGE2_ROWS = 1024
STAGE2_CHUNK = 512
VMEM_LIMIT = 48 * 1024 * 1024

BF16 = jnp.bfloat16
F32 = jnp.float32


def _params(*sem):
    return pltpu.CompilerParams(dimension_semantics=sem, vmem_limit_bytes=VMEM_LIMIT)


def _rms_scale(x, g):
    inv = lax.rsqrt(jnp.mean(x * x, axis=-1, keepdims=True) + RMS_EPS)
    return x * inv * g


def _silu(z):
    return z * (1.0 / (1.0 + jnp.exp(-z)))


def _t5_bucket_np(rel):
    half = NUM_BUCKETS // 2
    n = -rel
    ret = (n < 0).astype(np.int32) * half
    n = np.abs(n)
    max_exact = half // 2
    is_small = n < max_exact
    large = max_exact + (np.log(np.maximum(n, 1) / max_exact) / math.log(MAX_DISTANCE / max_exact)
                         * (half - max_exact)).astype(np.int32)
    large = np.minimum(large, half - 1)
    return (ret + np.where(is_small, n, large)).astype(np.int32)


@functools.lru_cache(maxsize=None)
def _band_tables():
    qi = np.arange(BLOCK)[:, None]
    kj = np.arange(3 * BLOCK)[None, :]
    rel = kj - BLOCK - qi
    band = np.abs(rel) <= WINDOW
    return _t5_bucket_np(rel), np.where(band, 0.0, MASK_VALUE).astype(np.float32)


@functools.lru_cache(maxsize=None)
def _channel_dft_table():
    c = np.arange(FGROUP_CH)
    ang = 2.0 * np.pi * ((c[:, None] * c[None, :]) % FGROUP_CH) / FGROUP_CH
    return (np.concatenate([np.cos(ang), np.sin(ang)], axis=1) / 16.0).astype(np.float32)


@functools.lru_cache(maxsize=None)
def _stage1_table(seq):
    k = np.arange(DFT_N1)
    ang = 2.0 * np.pi * ((k[:, None] * k[None, :]) % DFT_N1) / DFT_N1
    c, s = np.cos(ang), np.sin(ang)
    w = np.block([[c, -s], [-s, -c]]) / math.sqrt(seq)
    return w.astype(np.float32)


@functools.lru_cache(maxsize=None)
def _stage2_table(seq):
    n2_len = seq // DFT_N1
    k1 = np.arange(DFT_N1)[:, None, None]
    k2 = np.arange(n2_len)[None, :, None]
    n2 = np.arange(n2_len)[None, None, :]
    ang = 2.0 * np.pi * (((k1 + DFT_N1 * k2) * n2) % seq) / seq
    return np.concatenate([np.cos(ang), np.sin(ang)], axis=2).astype(np.float32)


def _inproj_kernel(x_ref, g_ref, wq_ref, wkv_ref, wg_ref, q_ref, kv_ref, sg_ref):
    h = _rms_scale(x_ref[...], g_ref[...]).astype(BF16)
    q = jnp.dot(h, wq_ref[...], preferred_element_type=F32)
    q_ref[...] = (q * (HEAD_DIM ** -0.5 * LOG2E)).astype(BF16)
    kv_ref[...] = jnp.dot(h, wkv_ref[...], preferred_element_type=F32).astype(BF16)
    gate = jnp.dot(h, wg_ref[...], preferred_element_type=F32)
    sg_ref[...] = _silu(gate).astype(BF16)


def _inproj(x2, g, wq, wkv, wg):
    t = x2.shape[0]
    row = lambda i: (i, 0)
    const = lambda i: (0, 0)
    return pl.pallas_call(
        _inproj_kernel,
        grid=(t // ROW_TILE,),
        in_specs=[
            pl.BlockSpec((ROW_TILE, D_MODEL), row),
            pl.BlockSpec((1, D_MODEL), const),
            pl.BlockSpec((D_MODEL, D_MODEL), const),
            pl.BlockSpec((D_MODEL, 2 * KV_WIDTH), const),
            pl.BlockSpec((D_MODEL, D_MODEL), const),
        ],
        out_specs=[
            pl.BlockSpec((ROW_TILE, D_MODEL), row),
            pl.BlockSpec((ROW_TILE, 2 * KV_WIDTH), row),
            pl.BlockSpec((ROW_TILE, D_MODEL), row),
        ],
        out_shape=[
            jax.ShapeDtypeStruct((t, D_MODEL), BF16),
            jax.ShapeDtypeStruct((t, 2 * KV_WIDTH), BF16),
            jax.ShapeDtypeStruct((t, D_MODEL), BF16),
        ],
        compiler_params=_params("parallel"),
        name="inproj",
    )(x2, g, wq, wkv, wg)


def _attn_kernel(sink_ref, q_ref, sg_ref, x_ref, kvp_ref, kvc_ref, kvn_ref, bias_ref, wo_ref, gf_ref,
                 out_ref, hn_ref, kvw_ref, kz_ref, vd_ref, s_ref, o_ref):
    tq = q_ref.shape[1]
    blocks_per_tile = tq // BLOCK
    n_blocks = pl.num_programs(1) * blocks_per_tile
    tile = pl.program_id(1)

    kvw_ref[0:BLOCK, :] = kvp_ref[0]
    kvw_ref[BLOCK:BLOCK + tq, :] = kvc_ref[0]
    kvw_ref[BLOCK + tq:, :] = kvn_ref[0]

    low = lax.broadcasted_iota(jnp.int32, (1, 2 * HEAD_DIM), 1) < HEAD_DIM
    for c in range(N_KV_HEADS // 2):
        kc = kvw_ref[:, 2 * c * HEAD_DIM:(2 * c + 2) * HEAD_DIM]
        vc = kvw_ref[:, KV_WIDTH + 2 * c * HEAD_DIM:KV_WIDTH + (2 * c + 2) * HEAD_DIM]
        kc_sw = jnp.concatenate([kc[:, HEAD_DIM:], kc[:, :HEAD_DIM]], axis=1)
        vc_sw = jnp.concatenate([vc[:, HEAD_DIM:], vc[:, :HEAD_DIM]], axis=1)
        zero = jnp.zeros_like(kc)
        for t, tab in enumerate((jnp.where(low, kc, zero), jnp.where(low, zero, kc_sw),
                                 jnp.where(low, kc_sw, zero), jnp.where(low, zero, kc))):
            kz_ref[:, (4 * c + t) * 2 * HEAD_DIM:(4 * c + t + 1) * 2 * HEAD_DIM] = tab
        vd_ref[:, (2 * c) * 2 * HEAD_DIM:(2 * c + 1) * 2 * HEAD_DIM] = jnp.where(low, vc, vc_sw)
        vd_ref[:, (2 * c + 1) * 2 * HEAD_DIM:(2 * c + 2) * 2 * HEAD_DIM] = jnp.where(low, vc_sw, vc)

    nt_dims = (((1,), (1,)), ((), ()))

    def scores(j, slot):
        r0 = j * BLOCK
        for hh in range(N_KV_HEADS):
            qp = q_ref[0, pl.ds(r0, BLOCK), hh * 4 * HEAD_DIM:(hh + 1) * 4 * HEAD_DIM]
            lhs = jnp.concatenate([qp[:, :2 * HEAD_DIM], qp[:, 2 * HEAD_DIM:]], axis=0)
            for par in range(2):
                kz = kz_ref[pl.ds(r0, 3 * BLOCK), (2 * hh + par) * 2 * HEAD_DIM:(2 * hh + par + 1) * 2 * HEAD_DIM]
                s2 = lax.dot_general(lhs, kz, nt_dims, preferred_element_type=F32)
                s_ref[slot, hh * GROUP + par] = s2[:BLOCK]
                s_ref[slot, hh * GROUP + 2 + par] = s2[BLOCK:]

    def finish(j, slot):
        r0 = j * BLOCK
        gb = tile * blocks_per_tile + j
        variant = jnp.where(gb == 0, 0, jnp.where(gb == n_blocks - 1, 2, 1))
        for hh in range(N_KV_HEADS):
            ps, inv_ls = [], []
            for g in range(GROUP):
                hd = hh * GROUP + g
                s = s_ref[slot, hd] + bias_ref[variant, hd]
                sk = sink_ref[hd]
                m = jnp.maximum(jnp.max(s, axis=-1, keepdims=True), sk)
                p = jnp.exp2(s - m)
                inv_ls.append(1.0 / (jnp.sum(p, axis=-1, keepdims=True) + jnp.exp2(sk - m)))
                ps.append(p.astype(BF16))
            vd = vd_ref[pl.ds(r0, 3 * BLOCK), hh * 2 * HEAD_DIM:(hh + 1) * 2 * HEAD_DIM]
            o2 = jnp.dot(jnp.concatenate(ps, axis=0), vd, preferred_element_type=F32)
            for t in range(2):
                even = o2[2 * t * BLOCK:(2 * t + 1) * BLOCK] * inv_ls[2 * t]
                odd = o2[(2 * t + 1) * BLOCK:(2 * t + 2) * BLOCK] * inv_ls[2 * t + 1]
                pair = 2 * hh + t
                o_ref[pl.ds(r0, BLOCK), pair * 2 * HEAD_DIM:(pair + 1) * 2 * HEAD_DIM] = jnp.where(low, even, odd)

    scores(0, 0)
    for j in range(blocks_per_tile):
        if j + 1 < blocks_per_tile:
            scores(j + 1, (j + 1) % 2)
        finish(j, j % 2)

    og = (o_ref[...] * sg_ref[0].astype(F32)).astype(BF16)
    x1 = x_ref[0] + jnp.dot(og, wo_ref[...], preferred_element_type=F32)
    out_ref[0] = x1
    hn_ref[0] = _rms_scale(x1, gf_ref[...]).astype(BF16)


def _attention(sink, q, kv, sg, x, bias, wo, gf):
    b, s, _ = x.shape
    tq = ROW_TILE
    r = tq // BLOCK
    last_block = s // BLOCK - 1
    tile3 = lambda bi, i: (bi, i, 0)
    return pl.pallas_call(
        _attn_kernel,
        grid=(b, s // tq),
        in_specs=[
            pl.BlockSpec(memory_space=pltpu.SMEM),
            pl.BlockSpec((1, tq, D_MODEL), tile3),
            pl.BlockSpec((1, tq, D_MODEL), tile3),
            pl.BlockSpec((1, tq, D_MODEL), tile3),
            pl.BlockSpec((1, BLOCK, 2 * KV_WIDTH), lambda bi, i: (bi, jnp.maximum(i * r - 1, 0), 0)),
            pl.BlockSpec((1, tq, 2 * KV_WIDTH), tile3),
            pl.BlockSpec((1, BLOCK, 2 * KV_WIDTH), lambda bi, i: (bi, jnp.minimum((i + 1) * r, last_block), 0)),
            pl.BlockSpec((3, N_HEADS, BLOCK, 3 * BLOCK), lambda bi, i: (0, 0, 0, 0),
                         pipeline_mode=pl.Buffered(1)),
            pl.BlockSpec((D_MODEL, D_MODEL), lambda bi, i: (0, 0), pipeline_mode=pl.Buffered(1)),
            pl.BlockSpec((1, D_MODEL), lambda bi, i: (0, 0)),
        ],
        out_specs=[pl.BlockSpec((1, tq, D_MODEL), tile3), pl.BlockSpec((1, tq, D_MODEL), tile3)],
        out_shape=[jax.ShapeDtypeStruct((b, s, D_MODEL), F32), jax.ShapeDtypeStruct((b, s, D_MODEL), BF16)],
        scratch_shapes=[
            pltpu.VMEM((tq + 2 * BLOCK, 2 * KV_WIDTH), BF16),
            pltpu.VMEM((tq + 2 * BLOCK, 2 * N_KV_HEADS * 2 * HEAD_DIM), BF16),
            pltpu.VMEM((tq + 2 * BLOCK, N_KV_HEADS * 2 * HEAD_DIM), BF16),
            pltpu.VMEM((2, N_HEADS, BLOCK, 3 * BLOCK), F32),
            pltpu.VMEM((tq, D_MODEL), F32),
        ],
        compiler_params=_params("parallel", "parallel"),
        name="attention",
    )(sink, q, sg, x, kv, kv, kv, bias, wo, gf)


def _stage1_kernel(w_ref, cs_ref, h_ref, y_ref):
    _, n1, tn2, d = h_ref.shape
    hs = jnp.swapaxes(h_ref[0], 0, 1).reshape(tn2 * n1, d)
    for grp in range(N_FGROUPS):
        lo, hi = grp * FGROUP_CH, (grp + 1) * FGROUP_CH
        t = jnp.dot(hs[:, lo:hi], cs_ref[...], preferred_element_type=F32).astype(BF16)
        ys = []
        for j in range(tn2):
            tj = t[j * n1:(j + 1) * n1]
            u = jnp.concatenate([tj[:, :FGROUP_CH], tj[:, FGROUP_CH:]], axis=0)
            ys.append(jnp.dot(w_ref[...], u, preferred_element_type=F32).astype(BF16))
        y = jnp.swapaxes(jnp.stack(ys, axis=0), 0, 1)
        y_ref[0, :, :, :, lo:hi] = y.reshape(2, n1, tn2, FGROUP_CH)


def _dft_stage1(w1, cs, h4):
    b, n1, n2, d = h4.shape
    const = lambda bi, i: (0, 0)
    return pl.pallas_call(
        _stage1_kernel,
        grid=(b, n2 // BF16_SUBLANES),
        in_specs=[
            pl.BlockSpec((2 * n1, 2 * n1), const),
            pl.BlockSpec((FGROUP_CH, 2 * FGROUP_CH), const),
            pl.BlockSpec((1, n1, BF16_SUBLANES, d), lambda bi, i: (bi, 0, i, 0)),
        ],
        out_specs=pl.BlockSpec((1, 2, n1, BF16_SUBLANES, d), lambda bi, i: (bi, 0, 0, i, 0)),
        out_shape=jax.ShapeDtypeStruct((b, 2, n1, n2, d), BF16),
        compiler_params=_params("parallel", "parallel"),
        name="dft_stage1",
    )(w1, cs, h4)


def _stage2_kernel(y_ref, m_ref, x_ref, gf_ref, wg_ref, wo_ref, gl_ref, out_ref, f_ref):
    tk1, n2 = m_ref.shape[0], m_ref.shape[1]
    fs = []
    for j in range(tk1):
        ycat = jnp.concatenate([y_ref[0, 0, j], y_ref[0, 1, j]], axis=0)
        fs.append(jnp.dot(m_ref[j], ycat, preferred_element_type=F32))
    f_ref[...] = jnp.swapaxes(jnp.stack(fs, axis=0), 0, 1)
    ck2 = STAGE2_CHUNK // tk1
    for c0 in range(0, n2, ck2):
        x = x_ref[0, c0:c0 + ck2].reshape(STAGE2_CHUNK, D_MODEL)
        h = _rms_scale(x, gf_ref[...]).astype(BF16)
        sg = _silu(jnp.dot(h, wg_ref[...], preferred_element_type=F32))
        fg = (f_ref[c0:c0 + ck2].reshape(STAGE2_CHUNK, D_MODEL) * sg).astype(BF16)
        xr = x + jnp.dot(fg, wo_ref[...], preferred_element_type=F32)
        out_ref[0, c0:c0 + ck2] = _rms_scale(xr, gl_ref[...]).reshape(ck2, tk1, D_MODEL)


def _dft_stage2(y5, mtab, x4, gf, wg, wo, gl):
    b, _, n1, n2, d = y5.shape
    tk1 = STAGE2_ROWS // n2
    rows = pl.BlockSpec((1, n2, tk1, d), lambda bi, i: (bi, 0, i, 0))
    const = lambda bi, i: (0, 0)
    return pl.pallas_call(
        _stage2_kernel,
        grid=(b, n1 // tk1),
        in_specs=[
            pl.BlockSpec((1, 2, tk1, n2, d), lambda bi, i: (bi, 0, i, 0, 0)),
            pl.BlockSpec((tk1, n2, 2 * n2), lambda bi, i: (i, 0, 0)),
            rows,
            pl.BlockSpec((1, d), const),
            pl.BlockSpec((d, d), const),
            pl.BlockSpec((d, d), const),
            pl.BlockSpec((1, d), const),
        ],
        out_specs=rows,
        out_shape=jax.ShapeDtypeStruct(x4.shape, F32),
        scratch_shapes=[pltpu.VMEM((n2, tk1, d), F32)],
        compiler_params=_params("parallel", "parallel"),
        name="dft_stage2",
    )(y5, mtab, x4, gf, wg, wo, gl)


def _trunk(x, bias, attn_norm, wq, wkv, wgate_a, wo_a, sink, fourier_norm, wgate_f, wo_f, final_norm, cs):
    b, s, d = x.shape
    n2 = s // DFT_N1
    q, kv, sg = _inproj(x.reshape(b * s, d), attn_norm, wq, wkv, wgate_a)
    x1, hn = _attention(sink, q.reshape(b, s, d), kv.reshape(b, s, 2 * KV_WIDTH), sg.reshape(b, s, d), x, bias,
                        wo_a, fourier_norm)
    w1 = jnp.asarray(_stage1_table(s)).astype(BF16)
    mtab = jnp.asarray(_stage2_table(s)).astype(BF16)
    y5 = _dft_stage1(w1, cs, hn.reshape(b, DFT_N1, n2, d))
    out = _dft_stage2(y5, mtab, x1.reshape(b, n2, DFT_N1, d), fourier_norm, wgate_f, wo_f, final_norm)
    return out.reshape(b, s, d)


def kernel(x_prompt, x_sample, rel_bias, attn_norm, attn_w_in, attn_w_out, attn_sink,
           fourier_norm, fourier_w_gate, fourier_w_out, final_norm):
    bucket, band_mask = _band_tables()
    onehot = (jnp.asarray(bucket.reshape(-1))[:, None] == jnp.arange(NUM_BUCKETS)[None, :]).astype(F32)
    bias = jnp.einsum("pb,bh->hp", onehot, rel_bias.astype(F32), precision=lax.Precision.HIGHEST)
    bias = bias.reshape(N_HEADS, BLOCK, 3 * BLOCK) * LOG2E + jnp.asarray(band_mask)[None]
    kcol = jnp.arange(3 * BLOCK)[None, None, :]
    bias = jnp.stack([jnp.where(kcol < BLOCK, MASK_VALUE, bias), bias,
                      jnp.where(kcol >= 2 * BLOCK, MASK_VALUE, bias)], axis=0)
    w_in = attn_w_in[0]
    wq = w_in[:, :D_MODEL].astype(BF16)
    wkv = w_in[:, D_MODEL:D_MODEL + 2 * KV_WIDTH].astype(BF16)
    wgate_a = w_in[:, D_MODEL + 2 * KV_WIDTH:].astype(BF16)
    cs = jnp.asarray(_channel_dft_table()).astype(BF16)
    args = (bias, attn_norm[0][None], wq, wkv, wgate_a, attn_w_out[0].astype(BF16), attn_sink[0] * LOG2E,
            fourier_norm[0][None], fourier_w_gate[0].astype(BF16), fourier_w_out[0].astype(BF16),
            final_norm[None], cs)
    return (_trunk(x_prompt, *args), _trunk(x_sample, *args))
```

```python
import functools
import math

import jax
import jax.numpy as jnp
import numpy as np
from jax import lax
from jax.experimental import pallas as pl
from jax.experimental.pallas import tpu as pltpu

D_MODEL = 1024
HEAD_DIM = 64
N_HEADS = 16
N_KV_HEADS = 4
GROUP = 4
KV_WIDTH = N_KV_HEADS * HEAD_DIM
WINDOW = 128
BLOCK = 128
NUM_BUCKETS = 32
MAX_DISTANCE = 128
N_FGROUPS = 4
FGROUP_CH = 256
RMS_EPS = 1e-6
MASK_VALUE = -1e30
LOG2E = math.log2(math.e)

DFT_N1 = 128
ROW_TILE = 1024
ATTN_ROWS = 512
PROJ_BLOCKS = 2
BF16_SUBLANES = 16
F32_SUBLANES = 8
STAGE2_ROWS = 1024
STAGE2_CHUNK = 512
VMEM_LIMIT = 48 * 1024 * 1024

BF16 = jnp.bfloat16
F32 = jnp.float32


def _params(*sem):
    return pltpu.CompilerParams(dimension_semantics=sem, vmem_limit_bytes=VMEM_LIMIT)


def _rms_scale(x, g):
    inv = lax.rsqrt(jnp.mean(x * x, axis=-1, keepdims=True) + RMS_EPS)
    return x * inv * g


def _silu(z):
    return z * (1.0 / (1.0 + jnp.exp(-z)))


def _t5_bucket_np(rel):
    half = NUM_BUCKETS // 2
    n = -rel
    ret = (n < 0).astype(np.int32) * half
    n = np.abs(n)
    max_exact = half // 2
    is_small = n < max_exact
    large = max_exact + (np.log(np.maximum(n, 1) / max_exact) / math.log(MAX_DISTANCE / max_exact)
                         * (half - max_exact)).astype(np.int32)
    large = np.minimum(large, half - 1)
    return (ret + np.where(is_small, n, large)).astype(np.int32)


@functools.lru_cache(maxsize=None)
def _band_tables():
    qi = np.arange(BLOCK)[:, None]
    kj = np.arange(3 * BLOCK)[None, :]
    rel = kj - BLOCK - qi
    band = np.abs(rel) <= WINDOW
    return _t5_bucket_np(rel), np.where(band, 0.0, MASK_VALUE).astype(np.float32)


@functools.lru_cache(maxsize=None)
def _channel_dft_table():
    c = np.arange(FGROUP_CH)
    ang = 2.0 * np.pi * ((c[:, None] * c[None, :]) % FGROUP_CH) / FGROUP_CH
    return (np.concatenate([np.cos(ang), np.sin(ang)], axis=1) / 16.0).astype(np.float32)


@functools.lru_cache(maxsize=None)
def _stage1_table(seq):
    k = np.arange(DFT_N1)
    ang = 2.0 * np.pi * ((k[:, None] * k[None, :]) % DFT_N1) / DFT_N1
    c, s = np.cos(ang), np.sin(ang)
    w = np.block([[c, -s], [-s, -c]]) / math.sqrt(seq)
    return w.astype(np.float32)


@functools.lru_cache(maxsize=None)
def _stage2_table(seq):
    n2_len = seq // DFT_N1
    k1 = np.arange(DFT_N1)[:, None, None]
    k2 = np.arange(n2_len)[None, :, None]
    n2 = np.arange(n2_len)[None, None, :]
    ang = 2.0 * np.pi * (((k1 + DFT_N1 * k2) * n2) % seq) / seq
    return np.concatenate([np.cos(ang), np.sin(ang)], axis=2).astype(np.float32)


def _inproj_kernel(x_ref, g_ref, wq_ref, wkv_ref, wg_ref, q_ref, kv_ref, sg_ref):
    h = _rms_scale(x_ref[...], g_ref[...]).astype(BF16)
    q = jnp.dot(h, wq_ref[...], preferred_element_type=F32)
    q_ref[...] = (q * (HEAD_DIM ** -0.5 * LOG2E)).astype(BF16)
    kv_ref[...] = jnp.dot(h, wkv_ref[...], preferred_element_type=F32).astype(BF16)
    gate = jnp.dot(h, wg_ref[...], preferred_element_type=F32)
    sg_ref[...] = _silu(gate).astype(BF16)


def _inproj(x2, g, wq, wkv, wg):
    t = x2.shape[0]
    row = lambda i: (i, 0)
    const = lambda i: (0, 0)
    return pl.pallas_call(
        _inproj_kernel,
        grid=(t // ROW_TILE,),
        in_specs=[
            pl.BlockSpec((ROW_TILE, D_MODEL), row),
            pl.BlockSpec((1, D_MODEL), const),
            pl.BlockSpec((D_MODEL, D_MODEL), const),
            pl.BlockSpec((D_MODEL, 2 * KV_WIDTH), const),
            pl.BlockSpec((D_MODEL, D_MODEL), const),
        ],
        out_specs=[
            pl.BlockSpec((ROW_TILE, D_MODEL), row),
            pl.BlockSpec((ROW_TILE, 2 * KV_WIDTH), row),
            pl.BlockSpec((ROW_TILE, D_MODEL), row),
        ],
        out_shape=[
            jax.ShapeDtypeStruct((t, D_MODEL), BF16),
            jax.ShapeDtypeStruct((t, 2 * KV_WIDTH), BF16),
            jax.ShapeDtypeStruct((t, D_MODEL), BF16),
        ],
        compiler_params=_params("parallel"),
        name="inproj",
    )(x2, g, wq, wkv, wg)


def _attn_kernel(sink_ref, q_ref, sg_ref, x_ref, kvp_ref, kvc_ref, kvn_ref, bias_ref, wo_ref, gf_ref,
                 out_ref, hn_ref, kvw_ref, kz_ref, vd_ref, s_ref, m_ref, o_ref):
    tq = q_ref.shape[1]
    blocks_per_tile = tq // BLOCK
    n_blocks = pl.num_programs(1) * blocks_per_tile
    tile = pl.program_id(1)

    kvw_ref[0:BLOCK, :] = kvp_ref[0]
    kvw_ref[BLOCK:BLOCK + tq, :] = kvc_ref[0]
    kvw_ref[BLOCK + tq:, :] = kvn_ref[0]

    low = lax.broadcasted_iota(jnp.int32, (1, 2 * HEAD_DIM), 1) < HEAD_DIM
    for c in range(N_KV_HEADS // 2):
        kc = kvw_ref[:, 2 * c * HEAD_DIM:(2 * c + 2) * HEAD_DIM]
        vc = kvw_ref[:, KV_WIDTH + 2 * c * HEAD_DIM:KV_WIDTH + (2 * c + 2) * HEAD_DIM]
        kc_sw = jnp.concatenate([kc[:, HEAD_DIM:], kc[:, :HEAD_DIM]], axis=1)
        vc_sw = jnp.concatenate([vc[:, HEAD_DIM:], vc[:, :HEAD_DIM]], axis=1)
        zero = jnp.zeros_like(kc)
        for t, tab in enumerate((jnp.where(low, kc, zero), jnp.where(low, zero, kc_sw),
                                 jnp.where(low, kc_sw, zero), jnp.where(low, zero, kc))):
            kz_ref[:, (4 * c + t) * 2 * HEAD_DIM:(4 * c + t + 1) * 2 * HEAD_DIM] = tab
        vd_ref[:, (2 * c) * 2 * HEAD_DIM:(2 * c + 1) * 2 * HEAD_DIM] = jnp.where(low, vc, vc_sw)
        vd_ref[:, (2 * c + 1) * 2 * HEAD_DIM:(2 * c + 2) * 2 * HEAD_DIM] = jnp.where(low, vc_sw, vc)

    nt_dims = (((1,), (1,)), ((), ()))

    def scores(j, slot):
        r0 = j * BLOCK
        gb = tile * blocks_per_tile + j
        variant = jnp.where(gb == 0, 0, jnp.where(gb == n_blocks - 1, 2, 1))
        for hh in range(N_KV_HEADS):
            qp = q_ref[0, pl.ds(r0, BLOCK), hh * 4 * HEAD_DIM:(hh + 1) * 4 * HEAD_DIM]
            lhs = jnp.concatenate([qp[:, :2 * HEAD_DIM], qp[:, 2 * HEAD_DIM:]], axis=0)
            for par in range(2):
                kz = kz_ref[pl.ds(r0, 3 * BLOCK), (2 * hh + par) * 2 * HEAD_DIM:(2 * hh + par + 1) * 2 * HEAD_DIM]
                s2 = lax.dot_general(lhs, kz, nt_dims, preferred_element_type=F32)
                for half in range(2):
                    hd = hh * GROUP + 2 * half + par
                    s = s2[half * BLOCK:(half + 1) * BLOCK] + bias_ref[variant, hd]
                    s_ref[slot, hd] = s
                    m = jnp.maximum(jnp.max(s, axis=-1, keepdims=True), sink_ref[hd])
                    m_ref[slot, hd] = jnp.broadcast_to(m, (BLOCK, BLOCK))

    def finish(j, slot):
        r0 = j * BLOCK
        for hh in range(N_KV_HEADS):
            ps, inv_ls = [], []
            for g in range(GROUP):
                hd = hh * GROUP + g
                m = m_ref[slot, hd]
                p = jnp.exp2(s_ref[slot, hd] - jnp.concatenate([m, m, m], axis=1))
                l = jnp.sum(p, axis=-1, keepdims=True) + jnp.exp2(sink_ref[hd] - m[:, :1])
                inv_ls.append(1.0 / l)
                ps.append(p.astype(BF16))
            vd = vd_ref[pl.ds(r0, 3 * BLOCK), hh * 2 * HEAD_DIM:(hh + 1) * 2 * HEAD_DIM]
            o2 = jnp.dot(jnp.concatenate(ps, axis=0), vd, preferred_element_type=F32)
            for t in range(2):
                even = o2[2 * t * BLOCK:(2 * t + 1) * BLOCK] * inv_ls[2 * t]
                odd = o2[(2 * t + 1) * BLOCK:(2 * t + 2) * BLOCK] * inv_ls[2 * t + 1]
                pair = 2 * hh + t
                o_ref[pl.ds(r0, BLOCK), pair * 2 * HEAD_DIM:(pair + 1) * 2 * HEAD_DIM] = jnp.where(low, even, odd)

    def project(r0, rows):
        og = (o_ref[r0:r0 + rows] * sg_ref[0, r0:r0 + rows].astype(F32)).astype(BF16)
        x1 = x_ref[0, r0:r0 + rows] + jnp.dot(og, wo_ref[...], preferred_element_type=F32)
        out_ref[0, r0:r0 + rows] = x1
        hn_ref[0, r0:r0 + rows] = _rms_scale(x1, gf_ref[...]).astype(BF16)

    scores(0, 0)
    for j in range(blocks_per_tile):
        if j + 1 < blocks_per_tile:
            scores(j + 1, (j + 1) % 2)
        finish(j, j % 2)
        if (j + 1) % PROJ_BLOCKS == 0:
            project((j + 1 - PROJ_BLOCKS) * BLOCK, PROJ_BLOCKS * BLOCK)


def _attention(sink, q, kv, sg, x, bias, wo, gf):
    b, s, _ = x.shape
    tq = ATTN_ROWS
    r = tq // BLOCK
    last_block = s // BLOCK - 1
    tile3 = lambda bi, i: (bi, i, 0)
    return pl.pallas_call(
        _attn_kernel,
        grid=(b, s // tq),
        in_specs=[
            pl.BlockSpec(memory_space=pltpu.SMEM),
            pl.BlockSpec((1, tq, D_MODEL), tile3),
            pl.BlockSpec((1, tq, D_MODEL), tile3),
            pl.BlockSpec((1, tq, D_MODEL), tile3),
            pl.BlockSpec((1, BLOCK, 2 * KV_WIDTH), lambda bi, i: (bi, jnp.maximum(i * r - 1, 0), 0)),
            pl.BlockSpec((1, tq, 2 * KV_WIDTH), tile3),
            pl.BlockSpec((1, BLOCK, 2 * KV_WIDTH), lambda bi, i: (bi, jnp.minimum((i + 1) * r, last_block), 0)),
            pl.BlockSpec((3, N_HEADS, BLOCK, 3 * BLOCK), lambda bi, i: (0, 0, 0, 0),
                         pipeline_mode=pl.Buffered(1)),
            pl.BlockSpec((D_MODEL, D_MODEL), lambda bi, i: (0, 0), pipeline_mode=pl.Buffered(1)),
            pl.BlockSpec((1, D_MODEL), lambda bi, i: (0, 0)),
        ],
        out_specs=[pl.BlockSpec((1, tq, D_MODEL), tile3), pl.BlockSpec((1, tq, D_MODEL), tile3)],
        out_shape=[jax.ShapeDtypeStruct((b, s, D_MODEL), F32), jax.ShapeDtypeStruct((b, s, D_MODEL), BF16)],
        scratch_shapes=[
            pltpu.VMEM((tq + 2 * BLOCK, 2 * KV_WIDTH), BF16),
            pltpu.VMEM((tq + 2 * BLOCK, 2 * N_KV_HEADS * 2 * HEAD_DIM), BF16),
            pltpu.VMEM((tq + 2 * BLOCK, N_KV_HEADS * 2 * HEAD_DIM), BF16),
            pltpu.VMEM((2, N_HEADS, BLOCK, 3 * BLOCK), F32),
            pltpu.VMEM((2, N_HEADS, BLOCK, BLOCK), F32),
            pltpu.VMEM((tq, D_MODEL), F32),
        ],
        compiler_params=_params("parallel", "parallel"),
        name="attention",
    )(sink, q, sg, x, kv, kv, kv, bias, wo, gf)


def _stage1_kernel(w_ref, cs_ref, h_ref, y_ref):
    _, n1, tn2, d = h_ref.shape
    hs = jnp.swapaxes(h_ref[0], 0, 1).reshape(tn2 * n1, d)
    for grp in range(N_FGROUPS):
        lo, hi = grp * FGROUP_CH, (grp + 1) * FGROUP_CH
        t = jnp.dot(hs[:, lo:hi], cs_ref[...], preferred_element_type=F32).astype(BF16)
        ys = []
        for j in range(tn2):
            tj = t[j * n1:(j + 1) * n1]
            u = jnp.concatenate([tj[:, :FGROUP_CH], tj[:, FGROUP_CH:]], axis=0)
            ys.append(jnp.dot(w_ref[...], u, preferred_element_type=F32).astype(BF16))
        y = jnp.swapaxes(jnp.stack(ys, axis=0), 0, 1)
        y_ref[0, :, :, :, lo:hi] = y.reshape(2, n1, tn2, FGROUP_CH)


def _dft_stage1(w1, cs, h4):
    b, n1, n2, d = h4.shape
    const = lambda bi, i: (0, 0)
    return pl.pallas_call(
        _stage1_kernel,
        grid=(b, n2 // BF16_SUBLANES),
        in_specs=[
            pl.BlockSpec((2 * n1, 2 * n1), const),
            pl.BlockSpec((FGROUP_CH, 2 * FGROUP_CH), const),
            pl.BlockSpec((1, n1, BF16_SUBLANES, d), lambda bi, i: (bi, 0, i, 0)),
        ],
        out_specs=pl.BlockSpec((1, 2, n1, BF16_SUBLANES, d), lambda bi, i: (bi, 0, 0, i, 0)),
        out_shape=jax.ShapeDtypeStruct((b, 2, n1, n2, d), BF16),
        compiler_params=_params("parallel", "parallel"),
        name="dft_stage1",
    )(w1, cs, h4)


def _stage2_kernel(y_ref, m_ref, x_ref, gf_ref, wg_ref, wo_ref, gl_ref, out_ref, f_ref):
    tk1, n2 = m_ref.shape[0], m_ref.shape[1]
    fs = []
    for j in range(tk1):
        ycat = jnp.concatenate([y_ref[0, 0, j], y_ref[0, 1, j]], axis=0)
        fs.append(jnp.dot(m_ref[j], ycat, preferred_element_type=F32))
    f_ref[...] = jnp.swapaxes(jnp.stack(fs, axis=0), 0, 1)
    ck2 = STAGE2_CHUNK // tk1
    for c0 in range(0, n2, ck2):
        x = x_ref[0, c0:c0 + ck2].reshape(STAGE2_CHUNK, D_MODEL)
        h = _rms_scale(x, gf_ref[...]).astype(BF16)
        sg = _silu(jnp.dot(h, wg_ref[...], preferred_element_type=F32))
        fg = (f_ref[c0:c0 + ck2].reshape(STAGE2_CHUNK, D_MODEL) * sg).astype(BF16)
        xr = x + jnp.dot(fg, wo_ref[...], preferred_element_type=F32)
        out_ref[0, c0:c0 + ck2] = _rms_scale(xr, gl_ref[...]).reshape(ck2, tk1, D_MODEL)


def _dft_stage2(y5, mtab, x4, gf, wg, wo, gl):
    b, _, n1, n2, d = y5.shape
    tk1 = STAGE2_ROWS // n2
    rows = pl.BlockSpec((1, n2, tk1, d), lambda bi, i: (bi, 0, i, 0))
    const = lambda bi, i: (0, 0)
    return pl.pallas_call(
        _stage2_kernel,
        grid=(b, n1 // tk1),
        in_specs=[
            pl.BlockSpec((1, 2, tk1, n2, d), lambda bi, i: (bi, 0, i, 0, 0)),
            pl.BlockSpec((tk1, n2, 2 * n2), lambda bi, i: (i, 0, 0)),
            rows,
            pl.BlockSpec((1, d), const),
            pl.BlockSpec((d, d), const),
            pl.BlockSpec((d, d), const),
            pl.BlockSpec((1, d), const),
        ],
        out_specs=rows,
        out_shape=jax.ShapeDtypeStruct(x4.shape, F32),
        scratch_shapes=[pltpu.VMEM((n2, tk1, d), F32)],
        compiler_params=_params("parallel", "parallel"),
        name="dft_stage2",
    )(y5, mtab, x4, gf, wg, wo, gl)


def _trunk(x, bias, attn_norm, wq, wkv, wgate_a, wo_a, sink, fourier_norm, wgate_f, wo_f, final_norm, cs):
    b, s, d = x.shape
    n2 = s // DFT_N1
    q, kv, sg = _inproj(x.reshape(b * s, d), attn_norm, wq, wkv, wgate_a)
    x1, hn = _attention(sink, q.reshape(b, s, d), kv.reshape(b, s, 2 * KV_WIDTH), sg.reshape(b, s, d), x, bias,
                        wo_a, fourier_norm)
    w1 = jnp.asarray(_stage1_table(s)).astype(BF16)
    mtab = jnp.asarray(_stage2_table(s)).astype(BF16)
    y5 = _dft_stage1(w1, cs, hn.reshape(b, DFT_N1, n2, d))
    out = _dft_stage2(y5, mtab, x1.reshape(b, n2, DFT_N1, d), fourier_norm, wgate_f, wo_f, final_norm)
    return out.reshape(b, s, d)


def kernel(x_prompt, x_sample, rel_bias, attn_norm, attn_w_in, attn_w_out, attn_sink,
           fourier_norm, fourier_w_gate, fourier_w_out, final_norm):
    bucket, band_mask = _band_tables()
    onehot = (jnp.asarray(bucket.reshape(-1))[:, None] == jnp.arange(NUM_BUCKETS)[None, :]).astype(F32)
    bias = jnp.einsum("pb,bh->hp", onehot, rel_bias.astype(F32), precision=lax.Precision.HIGHEST)
    bias = bias.reshape(N_HEADS, BLOCK, 3 * BLOCK) * LOG2E + jnp.asarray(band_mask)[None]
    kcol = jnp.arange(3 * BLOCK)[None, None, :]
    bias = jnp.stack([jnp.where(kcol < BLOCK, MASK_VALUE, bias), bias,
                      jnp.where(kcol >= 2 * BLOCK, MASK_VALUE, bias)], axis=0)
    w_in = attn_w_in[0]
    wq = w_in[:, :D_MODEL].astype(BF16)
    wkv = w_in[:, D_MODEL:D_MODEL + 2 * KV_WIDTH].astype(BF16)
    wgate_a = w_in[:, D_MODEL + 2 * KV_WIDTH:].astype(BF16)
    cs = jnp.asarray(_channel_dft_table()).astype(BF16)
    args = (bias, attn_norm[0][None], wq, wkv, wgate_a, attn_w_out[0].astype(BF16), attn_sink[0] * LOG2E,
            fourier_norm[0][None], fourier_w_gate[0].astype(BF16), fourier_w_out[0].astype(BF16),
            final_norm[None], cs)
    return (_trunk(x_prompt, *args), _trunk(x_sample, *args))
```

```python
import functools
import math

import jax
import jax.numpy as jnp
import numpy as np
from jax import lax
from jax.experimental import pallas as pl
from jax.experimental.pallas import tpu as pltpu

D_MODEL = 1024
HEAD_DIM = 64
N_HEADS = 16
N_KV_HEADS = 4
GROUP = 4
KV_WIDTH = N_KV_HEADS * HEAD_DIM
WINDOW = 128
BLOCK = 128
NUM_BUCKETS = 32
MAX_DISTANCE = 128
N_FGROUPS = 4
FGROUP_CH = 256
RMS_EPS = 1e-6
MASK_VALUE = -1e30
LOG2E = math.log2(math.e)

DFT_N1 = 128
ROW_TILE = 1024
ATTN_ROWS = 512
PROJ_BLOCKS = 2
BF16_SUBLANES = 16
F32_SUBLANES = 8
STAGE2_ROWS = 1024
STAGE2_CHUNK = 512
VMEM_LIMIT = 48 * 1024 * 1024

BF16 = jnp.bfloat16
F32 = jnp.float32


def _params(*sem, flags=None):
    return pltpu.CompilerParams(dimension_semantics=sem, vmem_limit_bytes=VMEM_LIMIT, flags=flags)


def _rms_scale(x, g):
    inv = lax.rsqrt(jnp.mean(x * x, axis=-1, keepdims=True) + RMS_EPS)
    return x * inv * g


def _silu(z):
    return z * (1.0 / (1.0 + jnp.exp(-z)))


def _t5_bucket_np(rel):
    half = NUM_BUCKETS // 2
    n = -rel
    ret = (n < 0).astype(np.int32) * half
    n = np.abs(n)
    max_exact = half // 2
    is_small = n < max_exact
    large = max_exact + (np.log(np.maximum(n, 1) / max_exact) / math.log(MAX_DISTANCE / max_exact)
                         * (half - max_exact)).astype(np.int32)
    large = np.minimum(large, half - 1)
    return (ret + np.where(is_small, n, large)).astype(np.int32)


@functools.lru_cache(maxsize=None)
def _band_tables():
    qi = np.arange(BLOCK)[:, None]
    kj = np.arange(3 * BLOCK)[None, :]
    rel = kj - BLOCK - qi
    band = np.abs(rel) <= WINDOW
    return _t5_bucket_np(rel), np.where(band, 0.0, MASK_VALUE).astype(np.float32)


@functools.lru_cache(maxsize=None)
def _channel_dft_table():
    c = np.arange(FGROUP_CH)
    ang = 2.0 * np.pi * ((c[:, None] * c[None, :]) % FGROUP_CH) / FGROUP_CH
    return (np.concatenate([np.cos(ang), np.sin(ang)], axis=1) / 16.0).astype(np.float32)


@functools.lru_cache(maxsize=None)
def _stage1_table(seq):
    k = np.arange(DFT_N1)
    ang = 2.0 * np.pi * ((k[:, None] * k[None, :]) % DFT_N1) / DFT_N1
    c, s = np.cos(ang), np.sin(ang)
    w = np.block([[c, -s], [-s, -c]]) / math.sqrt(seq)
    return w.astype(np.float32)


@functools.lru_cache(maxsize=None)
def _stage2_table(seq):
    n2_len = seq // DFT_N1
    k1 = np.arange(DFT_N1)[:, None, None]
    k2 = np.arange(n2_len)[None, :, None]
    n2 = np.arange(n2_len)[None, None, :]
    ang = 2.0 * np.pi * (((k1 + DFT_N1 * k2) * n2) % seq) / seq
    return np.concatenate([np.cos(ang), np.sin(ang)], axis=2).astype(np.float32)


def _inproj_kernel(x_ref, g_ref, wq_ref, wkv_ref, q_ref, kv_ref):
    h = _rms_scale(x_ref[...], g_ref[...]).astype(BF16)
    q = jnp.dot(h, wq_ref[...], preferred_element_type=F32)
    q_ref[...] = (q * (HEAD_DIM ** -0.5 * LOG2E)).astype(BF16)
    kv_ref[...] = jnp.dot(h, wkv_ref[...], preferred_element_type=F32).astype(BF16)


def _inproj(x2, g, wq, wkv):
    t = x2.shape[0]
    row = lambda i: (i, 0)
    const = lambda i: (0, 0)
    return pl.pallas_call(
        _inproj_kernel,
        grid=(t // ROW_TILE,),
        in_specs=[
            pl.BlockSpec((ROW_TILE, D_MODEL), row),
            pl.BlockSpec((1, D_MODEL), const),
            pl.BlockSpec((D_MODEL, D_MODEL), const),
            pl.BlockSpec((D_MODEL, 2 * KV_WIDTH), const),
        ],
        out_specs=[
            pl.BlockSpec((ROW_TILE, D_MODEL), row),
            pl.BlockSpec((ROW_TILE, 2 * KV_WIDTH), row),
        ],
        out_shape=[
            jax.ShapeDtypeStruct((t, D_MODEL), BF16),
            jax.ShapeDtypeStruct((t, 2 * KV_WIDTH), BF16),
        ],
        compiler_params=_params("parallel"),
        name="inproj",
    )(x2, g, wq, wkv)


def _attn_kernel(sink_ref, q_ref, x_ref, kvp_ref, kvc_ref, kvn_ref, bias_ref, ga_ref, wg_ref, wo_ref, gf_ref,
                 out_ref, hn_ref, kvw_ref, kz_ref, vt_ref, s_ref, m_ref, o_ref, sg_ref):
    tq = q_ref.shape[1]
    blocks_per_tile = tq // BLOCK
    n_blocks = pl.num_programs(1) * blocks_per_tile
    tile = pl.program_id(1)

    kvw_ref[0:BLOCK, :] = kvp_ref[0]
    kvw_ref[BLOCK:BLOCK + tq, :] = kvc_ref[0]
    kvw_ref[BLOCK + tq:, :] = kvn_ref[0]

    low = lax.broadcasted_iota(jnp.int32, (1, 2 * HEAD_DIM), 1) < HEAD_DIM
    for c in range(N_KV_HEADS // 2):
        kc = kvw_ref[:, 2 * c * HEAD_DIM:(2 * c + 2) * HEAD_DIM]
        kc_sw = jnp.concatenate([kc[:, HEAD_DIM:], kc[:, :HEAD_DIM]], axis=1)
        zero = jnp.zeros_like(kc)
        for t, tab in enumerate((jnp.where(low, kc, zero), jnp.where(low, zero, kc_sw),
                                 jnp.where(low, kc_sw, zero), jnp.where(low, zero, kc))):
            kz_ref[:, (4 * c + t) * 2 * HEAD_DIM:(4 * c + t + 1) * 2 * HEAD_DIM] = tab
    vt_ref[...] = kvw_ref[:, KV_WIDTH:].astype(F32).T.astype(BF16)

    nt_dims = (((1,), (1,)), ((), ()))
    first_half = lax.broadcasted_iota(jnp.int32, (1, 2 * BLOCK), 1) < BLOCK

    def sink_lanes(hh, par):
        return jnp.where(first_half, sink_ref[hh * GROUP + par], sink_ref[hh * GROUP + 2 + par])

    def scores(j, slot):
        r0 = j * BLOCK
        gb = tile * blocks_per_tile + j
        variant = jnp.where(gb == 0, 0, jnp.where(gb == n_blocks - 1, 2, 1))
        for hh in range(N_KV_HEADS):
            qp = q_ref[0, pl.ds(r0, BLOCK), hh * 4 * HEAD_DIM:(hh + 1) * 4 * HEAD_DIM]
            qs = jnp.concatenate([qp[:, :2 * HEAD_DIM], qp[:, 2 * HEAD_DIM:]], axis=0)
            for par in range(2):
                u = 2 * hh + par
                kz = kz_ref[pl.ds(r0, 3 * BLOCK), u * 2 * HEAD_DIM:(u + 1) * 2 * HEAD_DIM]
                bias = jnp.concatenate([bias_ref[variant, hh * GROUP + par],
                                        bias_ref[variant, hh * GROUP + 2 + par]], axis=1)
                s = lax.dot_general(kz, qs, nt_dims, preferred_element_type=F32) + bias
                s_ref[slot, u] = s
                m = jnp.maximum(jnp.max(s, axis=0, keepdims=True), sink_lanes(hh, par))
                m_ref[slot, u] = jnp.broadcast_to(m, (F32_SUBLANES, 2 * BLOCK))

    def finish(j, slot):
        r0 = j * BLOCK
        for hh in range(N_KV_HEADS):
            vt = vt_ref[hh * HEAD_DIM:(hh + 1) * HEAD_DIM, pl.ds(r0, 3 * BLOCK)]
            outs = []
            for par in range(2):
                u = 2 * hh + par
                m = m_ref[slot, u][0:1]
                p = jnp.exp2(s_ref[slot, u] - m)
                l = jnp.sum(p, axis=0, keepdims=True) + jnp.exp2(sink_lanes(hh, par) - m)
                o_t = jnp.dot(vt, p.astype(BF16), preferred_element_type=F32)
                outs.append(o_t * (1.0 / l))
            for t in range(2):
                pair_t = jnp.concatenate([outs[0][:, t * BLOCK:(t + 1) * BLOCK],
                                          outs[1][:, t * BLOCK:(t + 1) * BLOCK]], axis=0)
                pair = 2 * hh + t
                o_ref[pl.ds(r0, BLOCK), pair * 2 * HEAD_DIM:(pair + 1) * 2 * HEAD_DIM] = pair_t.T

    def project(r0, rows):
        og = (o_ref[r0:r0 + rows] * sg_ref[r0:r0 + rows]).astype(BF16)
        x1 = x_ref[0, r0:r0 + rows] + jnp.dot(og, wo_ref[...], preferred_element_type=F32)
        out_ref[0, r0:r0 + rows] = x1
        hn_ref[0, r0:r0 + rows] = _rms_scale(x1, gf_ref[...]).astype(BF16)

    def gate(r0, rows):
        h = _rms_scale(x_ref[0, r0:r0 + rows], ga_ref[...]).astype(BF16)
        sg_ref[r0:r0 + rows] = _silu(jnp.dot(h, wg_ref[...], preferred_element_type=F32))

    scores(0, 0)
    for j in range(blocks_per_tile):
        if j + 1 < blocks_per_tile:
            scores(j + 1, (j + 1) % 2)
        if j % PROJ_BLOCKS == 0:
            gate(j * BLOCK, PROJ_BLOCKS * BLOCK)
        finish(j, j % 2)
        if (j + 1) % PROJ_BLOCKS == 0:
            project((j + 1 - PROJ_BLOCKS) * BLOCK, PROJ_BLOCKS * BLOCK)


def _attention(sink, q, kv, x, bias, ga, wg, wo, gf):
    b, s, _ = x.shape
    tq = ATTN_ROWS
    r = tq // BLOCK
    last_block = s // BLOCK - 1
    tile3 = lambda bi, i: (bi, i, 0)
    vec = pl.BlockSpec((1, D_MODEL), lambda bi, i: (0, 0))
    weight = pl.BlockSpec((D_MODEL, D_MODEL), lambda bi, i: (0, 0), pipeline_mode=pl.Buffered(1))
    return pl.pallas_call(
        _attn_kernel,
        grid=(b, s // tq),
        in_specs=[
            pl.BlockSpec(memory_space=pltpu.SMEM),
            pl.BlockSpec((1, tq, D_MODEL), tile3),
            pl.BlockSpec((1, tq, D_MODEL), tile3),
            pl.BlockSpec((1, BLOCK, 2 * KV_WIDTH), lambda bi, i: (bi, jnp.maximum(i * r - 1, 0), 0)),
            pl.BlockSpec((1, tq, 2 * KV_WIDTH), tile3),
            pl.BlockSpec((1, BLOCK, 2 * KV_WIDTH), lambda bi, i: (bi, jnp.minimum((i + 1) * r, last_block), 0)),
            pl.BlockSpec((3, N_HEADS, 3 * BLOCK, BLOCK), lambda bi, i: (0, 0, 0, 0),
                         pipeline_mode=pl.Buffered(1)),
            vec,
            weight,
            weight,
            vec,
        ],
        out_specs=[pl.BlockSpec((1, tq, D_MODEL), tile3), pl.BlockSpec((1, tq, D_MODEL), tile3)],
        out_shape=[jax.ShapeDtypeStruct((b, s, D_MODEL), F32), jax.ShapeDtypeStruct((b, s, D_MODEL), BF16)],
        scratch_shapes=[
            pltpu.VMEM((tq + 2 * BLOCK, 2 * KV_WIDTH), BF16),
            pltpu.VMEM((tq + 2 * BLOCK, 2 * N_KV_HEADS * 2 * HEAD_DIM), BF16),
            pltpu.VMEM((KV_WIDTH, tq + 2 * BLOCK), BF16),
            pltpu.VMEM((2, N_HEADS // 2, 3 * BLOCK, 2 * BLOCK), F32),
            pltpu.VMEM((2, N_HEADS // 2, F32_SUBLANES, 2 * BLOCK), F32),
            pltpu.VMEM((tq, D_MODEL), F32),
            pltpu.VMEM((tq, D_MODEL), F32),
        ],
        compiler_params=_params("parallel", "parallel"),
        name="attention",
    )(sink, q, x, kv, kv, kv, bias, ga, wg, wo, gf)


def _stage1_kernel(w_ref, cs_ref, h_ref, y_ref):
    _, n1, tn2, d = h_ref.shape
    hs = jnp.swapaxes(h_ref[0], 0, 1).reshape(tn2 * n1, d)

    def channel_dft(grp):
        lo, hi = grp * FGROUP_CH, (grp + 1) * FGROUP_CH
        return jnp.dot(hs[:, lo:hi], cs_ref[...], preferred_element_type=F32).astype(BF16)

    t_next = channel_dft(0)
    for grp in range(N_FGROUPS):
        t = t_next
        ys = []
        for j in range(tn2):
            tj = t[j * n1:(j + 1) * n1]
            u = jnp.concatenate([tj[:, :FGROUP_CH], tj[:, FGROUP_CH:]], axis=0)
            ys.append(jnp.dot(w_ref[...], u, preferred_element_type=F32).astype(BF16))
        if grp + 1 < N_FGROUPS:
            t_next = channel_dft(grp + 1)
        y = jnp.swapaxes(jnp.stack(ys, axis=0), 0, 1)
        y_ref[0, :, :, :, grp * FGROUP_CH:(grp + 1) * FGROUP_CH] = y.reshape(2, n1, tn2, FGROUP_CH)


def _dft_stage1(w1, cs, h4):
    b, n1, n2, d = h4.shape
    const = lambda bi, i: (0, 0)
    return pl.pallas_call(
        _stage1_kernel,
        grid=(b, n2 // BF16_SUBLANES),
        in_specs=[
            pl.BlockSpec((2 * n1, 2 * n1), const),
            pl.BlockSpec((FGROUP_CH, 2 * FGROUP_CH), const),
            pl.BlockSpec((1, n1, BF16_SUBLANES, d), lambda bi, i: (bi, 0, i, 0)),
        ],
        out_specs=pl.BlockSpec((1, 2, n1, BF16_SUBLANES, d), lambda bi, i: (bi, 0, 0, i, 0)),
        out_shape=jax.ShapeDtypeStruct((b, 2, n1, n2, d), BF16),
        compiler_params=_params("parallel", "parallel"),
        name="dft_stage1",
    )(w1, cs, h4)


def _stage2_kernel(y_ref, m_ref, x_ref, gf_ref, wg_ref, wo_ref, gl_ref, out_ref, f_ref):
    tk1, n2 = m_ref.shape[0], m_ref.shape[1]
    fs = []
    for j in range(tk1):
        ycat = jnp.concatenate([y_ref[0, 0, j], y_ref[0, 1, j]], axis=0)
        fs.append(jnp.dot(m_ref[j], ycat, preferred_element_type=F32))
    f_ref[...] = jnp.swapaxes(jnp.stack(fs, axis=0), 0, 1)
    ck2 = STAGE2_CHUNK // tk1
    for c0 in range(0, n2, ck2):
        x = x_ref[0, c0:c0 + ck2].reshape(STAGE2_CHUNK, D_MODEL)
        h = _rms_scale(x, gf_ref[...]).astype(BF16)
        sg = _silu(jnp.dot(h, wg_ref[...], preferred_element_type=F32))
        fg = (f_ref[c0:c0 + ck2].reshape(STAGE2_CHUNK, D_MODEL) * sg).astype(BF16)
        xr = x + jnp.dot(fg, wo_ref[...], preferred_element_type=F32)
        out_ref[0, c0:c0 + ck2] = _rms_scale(xr, gl_ref[...]).reshape(ck2, tk1, D_MODEL)


def _dft_stage2(y5, mtab, x4, gf, wg, wo, gl):
    b, _, n1, n2, d = y5.shape
    tk1 = STAGE2_ROWS // n2
    rows = pl.BlockSpec((1, n2, tk1, d), lambda bi, i: (bi, 0, i, 0))
    const = lambda bi, i: (0, 0)
    return pl.pallas_call(
        _stage2_kernel,
        grid=(b, n1 // tk1),
        in_specs=[
            pl.BlockSpec((1, 2, tk1, n2, d), lambda bi, i: (bi, 0, i, 0, 0)),
            pl.BlockSpec((tk1, n2, 2 * n2), lambda bi, i: (i, 0, 0)),
            rows,
            pl.BlockSpec((1, d), const),
            pl.BlockSpec((d, d), const),
            pl.BlockSpec((d, d), const),
            pl.BlockSpec((1, d), const),
        ],
        out_specs=rows,
        out_shape=jax.ShapeDtypeStruct(x4.shape, F32),
        scratch_shapes=[pltpu.VMEM((n2, tk1, d), F32)],
        compiler_params=_params("parallel", "parallel"),
        name="dft_stage2",
    )(y5, mtab, x4, gf, wg, wo, gl)


def _trunk(x, bias, attn_norm, wq, wkv, wgate_a, wo_a, sink, fourier_norm, wgate_f, wo_f, final_norm, cs):
    b, s, d = x.shape
    n2 = s // DFT_N1
    q, kv = _inproj(x.reshape(b * s, d), attn_norm, wq, wkv)
    x1, hn = _attention(sink, q.reshape(b, s, d), kv.reshape(b, s, 2 * KV_WIDTH), x, bias,
                        attn_norm, wgate_a, wo_a, fourier_norm)
    w1 = jnp.asarray(_stage1_table(s)).astype(BF16)
    mtab = jnp.asarray(_stage2_table(s)).astype(BF16)
    y5 = _dft_stage1(w1, cs, hn.reshape(b, DFT_N1, n2, d))
    out = _dft_stage2(y5, mtab, x1.reshape(b, n2, DFT_N1, d), fourier_norm, wgate_f, wo_f, final_norm)
    return out.reshape(b, s, d)


def kernel(x_prompt, x_sample, rel_bias, attn_norm, attn_w_in, attn_w_out, attn_sink,
           fourier_norm, fourier_w_gate, fourier_w_out, final_norm):
    bucket, band_mask = _band_tables()
    onehot = (jnp.asarray(bucket.T.reshape(-1))[:, None] == jnp.arange(NUM_BUCKETS)[None, :]).astype(F32)
    bias = jnp.einsum("pb,bh->hp", onehot, rel_bias.astype(F32), precision=lax.Precision.HIGHEST)
    bias = bias.reshape(N_HEADS, 3 * BLOCK, BLOCK) * LOG2E + jnp.asarray(band_mask.T)[None]
    key = jnp.arange(3 * BLOCK)[None, :, None]
    bias = jnp.stack([jnp.where(key < BLOCK, MASK_VALUE, bias), bias,
                      jnp.where(key >= 2 * BLOCK, MASK_VALUE, bias)], axis=0)
    w_in = attn_w_in[0]
    wq = w_in[:, :D_MODEL].astype(BF16)
    wkv = w_in[:, D_MODEL:D_MODEL + 2 * KV_WIDTH].astype(BF16)
    wgate_a = w_in[:, D_MODEL + 2 * KV_WIDTH:].astype(BF16)
    cs = jnp.asarray(_channel_dft_table()).astype(BF16)
    args = (bias, attn_norm[0][None], wq, wkv, wgate_a, attn_w_out[0].astype(BF16), attn_sink[0] * LOG2E,
            fourier_norm[0][None], fourier_w_gate[0].astype(BF16), fourier_w_out[0].astype(BF16),
            final_norm[None], cs)
    return (_trunk(x_prompt, *args), _trunk(x_sample, *args))
```

```python
import functools
import math

import jax
import jax.numpy as jnp
import numpy as np
from jax import lax
from jax.experimental import pallas as pl
from jax.experimental.pallas import tpu as pltpu

D_MODEL = 1024
HEAD_DIM = 64
N_HEADS = 16
N_KV_HEADS = 4
GROUP = 4
KV_WIDTH = N_KV_HEADS * HEAD_DIM
WINDOW = 128
BLOCK = 128
NUM_BUCKETS = 32
MAX_DISTANCE = 128
N_FGROUPS = 4
FGROUP_CH = 256
RMS_EPS = 1e-6
MASK_VALUE = -1e30
LOG2E = math.log2(math.e)

DFT_N1 = 128
ROW_TILE = 1024
ATTN_ROWS = 512
VT_ROWS = HEAD_DIM + 16
PROJ_BLOCKS = 2
BF16_SUBLANES = 16
F32_SUBLANES = 8
STAGE2_ROWS = 1024
STAGE2_CHUNK = 512
VMEM_LIMIT = 48 * 1024 * 1024

BF16 = jnp.bfloat16
F32 = jnp.float32


def _params(*sem, flags=None):
    return pltpu.CompilerParams(dimension_semantics=sem, vmem_limit_bytes=VMEM_LIMIT, flags=flags)


def _rms_scale(x, g):
    inv = lax.rsqrt(jnp.mean(x * x, axis=-1, keepdims=True) + RMS_EPS)
    return x * inv * g


def _silu(z):
    hz = 0.5 * z
    return hz + hz * jnp.tanh(hz)


def _t5_bucket_np(rel):
    half = NUM_BUCKETS // 2
    n = -rel
    ret = (n < 0).astype(np.int32) * half
    n = np.abs(n)
    max_exact = half // 2
    is_small = n < max_exact
    large = max_exact + (np.log(np.maximum(n, 1) / max_exact) / math.log(MAX_DISTANCE / max_exact)
                         * (half - max_exact)).astype(np.int32)
    large = np.minimum(large, half - 1)
    return (ret + np.where(is_small, n, large)).astype(np.int32)


@functools.lru_cache(maxsize=None)
def _band_tables():
    qi = np.arange(BLOCK)[:, None]
    kj = np.arange(3 * BLOCK)[None, :]
    rel = kj - BLOCK - qi
    band = np.abs(rel) <= WINDOW
    return _t5_bucket_np(rel), np.where(band, 0.0, MASK_VALUE).astype(np.float32)


@functools.lru_cache(maxsize=None)
def _channel_dft_table():
    c = np.arange(FGROUP_CH)
    ang = 2.0 * np.pi * ((c[:, None] * c[None, :]) % FGROUP_CH) / FGROUP_CH
    return (np.concatenate([np.cos(ang), np.sin(ang)], axis=1) / 16.0).astype(np.float32)


@functools.lru_cache(maxsize=None)
def _stage1_table(seq):
    k = np.arange(DFT_N1)
    ang = 2.0 * np.pi * ((k[:, None] * k[None, :]) % DFT_N1) / DFT_N1
    c, s = np.cos(ang), np.sin(ang)
    w = np.block([[c, -s], [-s, -c]]) / math.sqrt(seq)
    return w.astype(np.float32)


@functools.lru_cache(maxsize=None)
def _stage2_table(seq):
    n2_len = seq // DFT_N1
    k1 = np.arange(DFT_N1)[:, None, None]
    k2 = np.arange(n2_len)[None, :, None]
    n2 = np.arange(n2_len)[None, None, :]
    ang = 2.0 * np.pi * (((k1 + DFT_N1 * k2) * n2) % seq) / seq
    return np.concatenate([np.cos(ang), np.sin(ang)], axis=2).astype(np.float32)


def _inproj_kernel(x_ref, g_ref, wq_ref, wkv_ref, q_ref, kv_ref):
    h = _rms_scale(x_ref[...], g_ref[...]).astype(BF16)
    q = jnp.dot(h, wq_ref[...], preferred_element_type=F32)
    q_ref[...] = (q * (HEAD_DIM ** -0.5 * LOG2E)).astype(BF16)
    kv_ref[...] = jnp.dot(h, wkv_ref[...], preferred_element_type=F32).astype(BF16)


def _inproj(x2, g, wq, wkv):
    t = x2.shape[0]
    row = lambda i: (i, 0)
    const = lambda i: (0, 0)
    return pl.pallas_call(
        _inproj_kernel,
        grid=(t // ROW_TILE,),
        in_specs=[
            pl.BlockSpec((ROW_TILE, D_MODEL), row),
            pl.BlockSpec((1, D_MODEL), const),
            pl.BlockSpec((D_MODEL, D_MODEL), const),
            pl.BlockSpec((D_MODEL, 2 * KV_WIDTH), const),
        ],
        out_specs=[
            pl.BlockSpec((ROW_TILE, D_MODEL), row),
            pl.BlockSpec((ROW_TILE, 2 * KV_WIDTH), row),
        ],
        out_shape=[
            jax.ShapeDtypeStruct((t, D_MODEL), BF16),
            jax.ShapeDtypeStruct((t, 2 * KV_WIDTH), BF16),
        ],
        compiler_params=_params("parallel"),
        name="inproj",
    )(x2, g, wq, wkv)


def _attn_kernel(sink_ref, q_ref, x_ref, kvp_ref, kvc_ref, kvn_ref, bias_ref, ga_ref, wg_ref, wo_ref, gf_ref,
                 out_ref, hn_ref, kvw_ref, kz_ref, vt_ref, s_ref, m_ref, o_ref, sg_ref):
    tq = q_ref.shape[1]
    blocks_per_tile = tq // BLOCK
    n_blocks = pl.num_programs(1) * blocks_per_tile
    tile = pl.program_id(1)

    kvw_ref[0:BLOCK, :] = kvp_ref[0]
    kvw_ref[BLOCK:BLOCK + tq, :] = kvc_ref[0]
    kvw_ref[BLOCK + tq:, :] = kvn_ref[0]

    low = lax.broadcasted_iota(jnp.int32, (1, 2 * HEAD_DIM), 1) < HEAD_DIM
    for c in range(N_KV_HEADS // 2):
        kc = kvw_ref[:, 2 * c * HEAD_DIM:(2 * c + 2) * HEAD_DIM]
        kc_sw = jnp.concatenate([kc[:, HEAD_DIM:], kc[:, :HEAD_DIM]], axis=1)
        zero = jnp.zeros_like(kc)
        for t, tab in enumerate((jnp.where(low, kc, zero), jnp.where(low, zero, kc_sw),
                                 jnp.where(low, kc_sw, zero), jnp.where(low, zero, kc))):
            kz_ref[:, (4 * c + t) * 2 * HEAD_DIM:(4 * c + t + 1) * 2 * HEAD_DIM] = tab
    v_t = kvw_ref[:, KV_WIDTH:].astype(F32).T.astype(BF16)
    for hh in range(N_KV_HEADS):
        vt_ref[hh * VT_ROWS:hh * VT_ROWS + HEAD_DIM] = v_t[hh * HEAD_DIM:(hh + 1) * HEAD_DIM]
        vt_ref[hh * VT_ROWS + HEAD_DIM:(hh + 1) * VT_ROWS] = jnp.ones((VT_ROWS - HEAD_DIM, tq + 2 * BLOCK), BF16)

    nt_dims = (((1,), (1,)), ((), ()))
    first_half = lax.broadcasted_iota(jnp.int32, (1, 2 * BLOCK), 1) < BLOCK

    def sink_lanes(hh, par):
        return jnp.where(first_half, sink_ref[hh * GROUP + par], sink_ref[hh * GROUP + 2 + par])

    def scores(j, slot):
        r0 = j * BLOCK
        gb = tile * blocks_per_tile + j
        variant = jnp.where(gb == 0, 0, jnp.where(gb == n_blocks - 1, 2, 1))
        for hh in range(N_KV_HEADS):
            qp = q_ref[0, pl.ds(r0, BLOCK), hh * 4 * HEAD_DIM:(hh + 1) * 4 * HEAD_DIM]
            qs = jnp.concatenate([qp[:, :2 * HEAD_DIM], qp[:, 2 * HEAD_DIM:]], axis=0)
            for par in range(2):
                u = 2 * hh + par
                kz = kz_ref[pl.ds(r0, 3 * BLOCK), u * 2 * HEAD_DIM:(u + 1) * 2 * HEAD_DIM]
                bias = jnp.concatenate([bias_ref[variant, hh * GROUP + par],
                                        bias_ref[variant, hh * GROUP + 2 + par]], axis=1)
                s = lax.dot_general(kz, qs, nt_dims, preferred_element_type=F32) + bias
                s_ref[slot, u] = s
                m = jnp.maximum(jnp.max(s, axis=0, keepdims=True), sink_lanes(hh, par))
                m_ref[slot, u] = jnp.broadcast_to(m, (F32_SUBLANES, 2 * BLOCK))

    def finish(j, slot):
        r0 = j * BLOCK
        for hh in range(N_KV_HEADS):
            vt = vt_ref[hh * VT_ROWS:(hh + 1) * VT_ROWS, pl.ds(r0, 3 * BLOCK)]
            outs = []
            for par in range(2):
                u = 2 * hh + par
                m = m_ref[slot, u][0:1]
                p = jnp.exp2(s_ref[slot, u] - m)
                o_t = jnp.dot(vt, p.astype(BF16), preferred_element_type=F32)
                l = o_t[HEAD_DIM:HEAD_DIM + 1] + jnp.exp2(sink_lanes(hh, par) - m)
                outs.append(o_t[:HEAD_DIM] * (1.0 / l))
            for t in range(2):
                pair_t = jnp.concatenate([outs[0][:, t * BLOCK:(t + 1) * BLOCK],
                                          outs[1][:, t * BLOCK:(t + 1) * BLOCK]], axis=0)
                pair = 2 * hh + t
                o_ref[pl.ds(r0, BLOCK), pair * 2 * HEAD_DIM:(pair + 1) * 2 * HEAD_DIM] = pair_t.T

    def project(r0, rows):
        og = (o_ref[r0:r0 + rows] * sg_ref[r0:r0 + rows]).astype(BF16)
        x1 = x_ref[0, r0:r0 + rows] + jnp.dot(og, wo_ref[...], preferred_element_type=F32)
        out_ref[0, r0:r0 + rows] = x1
        hn_ref[0, r0:r0 + rows] = _rms_scale(x1, gf_ref[...]).astype(BF16)

    def gate(r0, rows):
        h = _rms_scale(x_ref[0, r0:r0 + rows], ga_ref[...]).astype(BF16)
        sg_ref[r0:r0 + rows] = _silu(jnp.dot(h, wg_ref[...], preferred_element_type=F32))

    scores(0, 0)
    for j in range(blocks_per_tile):
        if j + 1 < blocks_per_tile:
            scores(j + 1, (j + 1) % 2)
        if j % PROJ_BLOCKS == 0:
            gate(j * BLOCK, PROJ_BLOCKS * BLOCK)
        finish(j, j % 2)
        if (j + 1) % PROJ_BLOCKS == 0:
            project((j + 1 - PROJ_BLOCKS) * BLOCK, PROJ_BLOCKS * BLOCK)


def _attention(sink, q, kv, x, bias, ga, wg, wo, gf):
    b, s, _ = x.shape
    tq = ATTN_ROWS
    r = tq // BLOCK
    last_block = s // BLOCK - 1
    tile3 = lambda bi, i: (bi, i, 0)
    vec = pl.BlockSpec((1, D_MODEL), lambda bi, i: (0, 0))
    weight = pl.BlockSpec((D_MODEL, D_MODEL), lambda bi, i: (0, 0), pipeline_mode=pl.Buffered(1))
    return pl.pallas_call(
        _attn_kernel,
        grid=(b, s // tq),
        in_specs=[
            pl.BlockSpec(memory_space=pltpu.SMEM),
            pl.BlockSpec((1, tq, D_MODEL), tile3),
            pl.BlockSpec((1, tq, D_MODEL), tile3),
            pl.BlockSpec((1, BLOCK, 2 * KV_WIDTH), lambda bi, i: (bi, jnp.maximum(i * r - 1, 0), 0)),
            pl.BlockSpec((1, tq, 2 * KV_WIDTH), tile3),
            pl.BlockSpec((1, BLOCK, 2 * KV_WIDTH), lambda bi, i: (bi, jnp.minimum((i + 1) * r, last_block), 0)),
            pl.BlockSpec((3, N_HEADS, 3 * BLOCK, BLOCK), lambda bi, i: (0, 0, 0, 0),
                         pipeline_mode=pl.Buffered(1)),
            vec,
            weight,
            weight,
            vec,
        ],
        out_specs=[pl.BlockSpec((1, tq, D_MODEL), tile3), pl.BlockSpec((1, tq, D_MODEL), tile3)],
        out_shape=[jax.ShapeDtypeStruct((b, s, D_MODEL), F32), jax.ShapeDtypeStruct((b, s, D_MODEL), BF16)],
        scratch_shapes=[
            pltpu.VMEM((tq + 2 * BLOCK, 2 * KV_WIDTH), BF16),
            pltpu.VMEM((tq + 2 * BLOCK, 2 * N_KV_HEADS * 2 * HEAD_DIM), BF16),
            pltpu.VMEM((N_KV_HEADS * VT_ROWS, tq + 2 * BLOCK), BF16),
            pltpu.VMEM((2, N_HEADS // 2, 3 * BLOCK, 2 * BLOCK), F32),
            pltpu.VMEM((2, N_HEADS // 2, F32_SUBLANES, 2 * BLOCK), F32),
            pltpu.VMEM((tq, D_MODEL), F32),
            pltpu.VMEM((tq, D_MODEL), F32),
        ],
        compiler_params=_params("parallel", "parallel"),
        name="attention",
    )(sink, q, x, kv, kv, kv, bias, ga, wg, wo, gf)


def _stage1_kernel(w_ref, cs_ref, h_ref, y_ref):
    _, n1, tn2, d = h_ref.shape
    hs = jnp.swapaxes(h_ref[0], 0, 1).reshape(tn2 * n1, d)

    def channel_dft(grp):
        lo, hi = grp * FGROUP_CH, (grp + 1) * FGROUP_CH
        return jnp.dot(hs[:, lo:hi], cs_ref[...], preferred_element_type=F32).astype(BF16)

    t_next = channel_dft(0)
    for grp in range(N_FGROUPS):
        t = t_next
        ys = []
        for j in range(tn2):
            tj = t[j * n1:(j + 1) * n1]
            u = jnp.concatenate([tj[:, :FGROUP_CH], tj[:, FGROUP_CH:]], axis=0)
            ys.append(jnp.dot(w_ref[...], u, preferred_element_type=F32).astype(BF16))
        if grp + 1 < N_FGROUPS:
            t_next = channel_dft(grp + 1)
        y = jnp.swapaxes(jnp.stack(ys, axis=0), 0, 1)
        y_ref[0, :, :, :, grp * FGROUP_CH:(grp + 1) * FGROUP_CH] = y.reshape(2, n1, tn2, FGROUP_CH)


def _dft_stage1(w1, cs, h4):
    b, n1, n2, d = h4.shape
    const = lambda bi, i: (0, 0)
    return pl.pallas_call(
        _stage1_kernel,
        grid=(b, n2 // BF16_SUBLANES),
        in_specs=[
            pl.BlockSpec((2 * n1, 2 * n1), const),
            pl.BlockSpec((FGROUP_CH, 2 * FGROUP_CH), const),
            pl.BlockSpec((1, n1, BF16_SUBLANES, d), lambda bi, i: (bi, 0, i, 0)),
        ],
        out_specs=pl.BlockSpec((1, 2, n1, BF16_SUBLANES, d), lambda bi, i: (bi, 0, 0, i, 0)),
        out_shape=jax.ShapeDtypeStruct((b, 2, n1, n2, d), BF16),
        compiler_params=_params("parallel", "parallel"),
        name="dft_stage1",
    )(w1, cs, h4)


def _stage2_kernel(y_ref, m_ref, x_ref, gf_ref, wg_ref, wo_ref, gl_ref, out_ref, f_ref):
    tk1, n2 = m_ref.shape[0], m_ref.shape[1]
    fs = []
    for j in range(tk1):
        ycat = jnp.concatenate([y_ref[0, 0, j], y_ref[0, 1, j]], axis=0)
        fs.append(jnp.dot(m_ref[j], ycat, preferred_element_type=F32))
    f_ref[...] = jnp.swapaxes(jnp.stack(fs, axis=0), 0, 1)
    ck2 = STAGE2_CHUNK // tk1
    for c0 in range(0, n2, ck2):
        x = x_ref[0, c0:c0 + ck2].reshape(STAGE2_CHUNK, D_MODEL)
        h = _rms_scale(x, gf_ref[...]).astype(BF16)
        sg = _silu(jnp.dot(h, wg_ref[...], preferred_element_type=F32))
        fg = (f_ref[c0:c0 + ck2].reshape(STAGE2_CHUNK, D_MODEL) * sg).astype(BF16)
        xr = x + jnp.dot(fg, wo_ref[...], preferred_element_type=F32)
        out_ref[0, c0:c0 + ck2] = _rms_scale(xr, gl_ref[...]).reshape(ck2, tk1, D_MODEL)


def _dft_stage2(y5, mtab, x4, gf, wg, wo, gl):
    b, _, n1, n2, d = y5.shape
    tk1 = STAGE2_ROWS // n2
    rows = pl.BlockSpec((1, n2, tk1, d), lambda bi, i: (bi, 0, i, 0))
    const = lambda bi, i: (0, 0)
    return pl.pallas_call(
        _stage2_kernel,
        grid=(b, n1 // tk1),
        in_specs=[
            pl.BlockSpec((1, 2, tk1, n2, d), lambda bi, i: (bi, 0, i, 0, 0)),
            pl.BlockSpec((tk1, n2, 2 * n2), lambda bi, i: (i, 0, 0)),
            rows,
            pl.BlockSpec((1, d), const),
            pl.BlockSpec((d, d), const),
            pl.BlockSpec((d, d), const),
            pl.BlockSpec((1, d), const),
        ],
        out_specs=rows,
        out_shape=jax.ShapeDtypeStruct(x4.shape, F32),
        scratch_shapes=[pltpu.VMEM((n2, tk1, d), F32)],
        compiler_params=_params("parallel", "parallel"),
        name="dft_stage2",
    )(y5, mtab, x4, gf, wg, wo, gl)


def _trunk(x, bias, attn_norm, wq, wkv, wgate_a, wo_a, sink, fourier_norm, wgate_f, wo_f, final_norm, cs):
    b, s, d = x.shape
    n2 = s // DFT_N1
    q, kv = _inproj(x.reshape(b * s, d), attn_norm, wq, wkv)
    x1, hn = _attention(sink, q.reshape(b, s, d), kv.reshape(b, s, 2 * KV_WIDTH), x, bias,
                        attn_norm, wgate_a, wo_a, fourier_norm)
    w1 = jnp.asarray(_stage1_table(s)).astype(BF16)
    mtab = jnp.asarray(_stage2_table(s)).astype(BF16)
    y5 = _dft_stage1(w1, cs, hn.reshape(b, DFT_N1, n2, d))
    out = _dft_stage2(y5, mtab, x1.reshape(b, n2, DFT_N1, d), fourier_norm, wgate_f, wo_f, final_norm)
    return out.reshape(b, s, d)


def kernel(x_prompt, x_sample, rel_bias, attn_norm, attn_w_in, attn_w_out, attn_sink,
           fourier_norm, fourier_w_gate, fourier_w_out, final_norm):
    bucket, band_mask = _band_tables()
    onehot = (jnp.asarray(bucket.T.reshape(-1))[:, None] == jnp.arange(NUM_BUCKETS)[None, :]).astype(F32)
    bias = jnp.einsum("pb,bh->hp", onehot, rel_bias.astype(F32), precision=lax.Precision.HIGHEST)
    bias = bias.reshape(N_HEADS, 3 * BLOCK, BLOCK) * LOG2E + jnp.asarray(band_mask.T)[None]
    key = jnp.arange(3 * BLOCK)[None, :, None]
    bias = jnp.stack([jnp.where(key < BLOCK, MASK_VALUE, bias), bias,
                      jnp.where(key >= 2 * BLOCK, MASK_VALUE, bias)], axis=0)
    w_in = attn_w_in[0]
    wq = w_in[:, :D_MODEL].astype(BF16)
    wkv = w_in[:, D_MODEL:D_MODEL + 2 * KV_WIDTH].astype(BF16)
    wgate_a = w_in[:, D_MODEL + 2 * KV_WIDTH:].astype(BF16)
    cs = jnp.asarray(_channel_dft_table()).astype(BF16)
    args = (bias, attn_norm[0][None], wq, wkv, wgate_a, attn_w_out[0].astype(BF16), attn_sink[0] * LOG2E,
            fourier_norm[0][None], fourier_w_gate[0].astype(BF16), fourier_w_out[0].astype(BF16),
            final_norm[None], cs)
    return (_trunk(x_prompt, *args), _trunk(x_sample, *args))
```

```python
import functools
import math

import jax
import jax.numpy as jnp
import numpy as np
from jax import lax
from jax.experimental import pallas as pl
from jax.experimental.pallas import tpu as pltpu

D_MODEL = 1024
HEAD_DIM = 64
N_HEADS = 16
N_KV_HEADS = 4
GROUP = 4
KV_WIDTH = N_KV_HEADS * HEAD_DIM
WINDOW = 128
BLOCK = 128
NUM_BUCKETS = 32
MAX_DISTANCE = 128
N_FGROUPS = 4
FGROUP_CH = 256
RMS_EPS = 1e-6
MASK_VALUE = -1e30
LOG2E = math.log2(math.e)

DFT_N1 = 128
ROW_TILE = 1024
ATTN_ROWS = 512
VT_ROWS = HEAD_DIM + 16
PROJ_BLOCKS = 2
BF16_SUBLANES = 16
F32_SUBLANES = 8
STAGE2_ROWS = 1024
STAGE2_CHUNK = 512
VMEM_LIMIT = 48 * 1024 * 1024

BF16 = jnp.bfloat16
F32 = jnp.float32


def _params(*sem, flags=None):
    return pltpu.CompilerParams(dimension_semantics=sem, vmem_limit_bytes=VMEM_LIMIT, flags=flags)


def _rms_scale(x, g):
    inv = lax.rsqrt(jnp.mean(x * x, axis=-1, keepdims=True) + RMS_EPS)
    return x * inv * g


def _silu(z):
    hz = 0.5 * z
    return hz + hz * jnp.tanh(hz)


def _t5_bucket_np(rel):
    half = NUM_BUCKETS // 2
    n = -rel
    ret = (n < 0).astype(np.int32) * half
    n = np.abs(n)
    max_exact = half // 2
    is_small = n < max_exact
    large = max_exact + (np.log(np.maximum(n, 1) / max_exact) / math.log(MAX_DISTANCE / max_exact)
                         * (half - max_exact)).astype(np.int32)
    large = np.minimum(large, half - 1)
    return (ret + np.where(is_small, n, large)).astype(np.int32)


@functools.lru_cache(maxsize=None)
def _band_tables():
    qi = np.arange(BLOCK)[:, None]
    kj = np.arange(3 * BLOCK)[None, :]
    rel = kj - BLOCK - qi
    band = np.abs(rel) <= WINDOW
    return _t5_bucket_np(rel), np.where(band, 0.0, MASK_VALUE).astype(np.float32)


@functools.lru_cache(maxsize=None)
def _channel_dft_table():
    c = np.arange(FGROUP_CH)
    ang = 2.0 * np.pi * ((c[:, None] * c[None, :]) % FGROUP_CH) / FGROUP_CH
    return (np.concatenate([np.cos(ang), np.sin(ang)], axis=1) / 16.0).astype(np.float32)


@functools.lru_cache(maxsize=None)
def _stage1_table(seq):
    k = np.arange(DFT_N1)
    ang = 2.0 * np.pi * ((k[:, None] * k[None, :]) % DFT_N1) / DFT_N1
    c, s = np.cos(ang), np.sin(ang)
    w = np.block([[c, -s], [-s, -c]]) / math.sqrt(seq)
    return w.astype(np.float32)


@functools.lru_cache(maxsize=None)
def _stage2_table(seq):
    n2_len = seq // DFT_N1
    k1 = np.arange(DFT_N1)[:, None, None]
    k2 = np.arange(n2_len)[None, :, None]
    n2 = np.arange(n2_len)[None, None, :]
    ang = 2.0 * np.pi * (((k1 + DFT_N1 * k2) * n2) % seq) / seq
    return np.concatenate([np.cos(ang), np.sin(ang)], axis=2).astype(np.float32)


def _inproj_kernel(x_ref, g_ref, wq_ref, wkv_ref, q_ref, kv_ref, h_ref):
    h = _rms_scale(x_ref[...], g_ref[...]).astype(BF16)
    h_ref[...] = h
    q = jnp.dot(h, wq_ref[...], preferred_element_type=F32)
    q_ref[...] = (q * (HEAD_DIM ** -0.5 * LOG2E)).astype(BF16)
    kv_ref[...] = jnp.dot(h, wkv_ref[...], preferred_element_type=F32).astype(BF16)


def _inproj(x2, g, wq, wkv):
    t = x2.shape[0]
    row = lambda i: (i, 0)
    const = lambda i: (0, 0)
    return pl.pallas_call(
        _inproj_kernel,
        grid=(t // ROW_TILE,),
        in_specs=[
            pl.BlockSpec((ROW_TILE, D_MODEL), row),
            pl.BlockSpec((1, D_MODEL), const),
            pl.BlockSpec((D_MODEL, D_MODEL), const),
            pl.BlockSpec((D_MODEL, 2 * KV_WIDTH), const),
        ],
        out_specs=[
            pl.BlockSpec((ROW_TILE, D_MODEL), row),
            pl.BlockSpec((ROW_TILE, 2 * KV_WIDTH), row),
            pl.BlockSpec((ROW_TILE, D_MODEL), row),
        ],
        out_shape=[
            jax.ShapeDtypeStruct((t, D_MODEL), BF16),
            jax.ShapeDtypeStruct((t, 2 * KV_WIDTH), BF16),
            jax.ShapeDtypeStruct((t, D_MODEL), BF16),
        ],
        compiler_params=_params("parallel"),
        name="inproj",
    )(x2, g, wq, wkv)


def _attn_kernel(sink_ref, q_ref, h_ref, x_ref, kvp_ref, kvc_ref, kvn_ref, bias_ref, wg_ref, wo_ref, gf_ref,
                 out_ref, hn_ref, kvw_ref, kz_ref, vt_ref, s_ref, m_ref, o_ref, sg_ref):
    tq = q_ref.shape[1]
    blocks_per_tile = tq // BLOCK
    n_blocks = pl.num_programs(1) * blocks_per_tile
    tile = pl.program_id(1)

    kvw_ref[0:BLOCK, :] = kvp_ref[0]
    kvw_ref[BLOCK:BLOCK + tq, :] = kvc_ref[0]
    kvw_ref[BLOCK + tq:, :] = kvn_ref[0]

    low = lax.broadcasted_iota(jnp.int32, (1, 2 * HEAD_DIM), 1) < HEAD_DIM
    for c in range(N_KV_HEADS // 2):
        kc = kvw_ref[:, 2 * c * HEAD_DIM:(2 * c + 2) * HEAD_DIM]
        kc_sw = jnp.concatenate([kc[:, HEAD_DIM:], kc[:, :HEAD_DIM]], axis=1)
        zero = jnp.zeros_like(kc)
        for t, tab in enumerate((jnp.where(low, kc, zero), jnp.where(low, zero, kc_sw),
                                 jnp.where(low, kc_sw, zero), jnp.where(low, zero, kc))):
            kz_ref[:, (4 * c + t) * 2 * HEAD_DIM:(4 * c + t + 1) * 2 * HEAD_DIM] = tab
    v_t = kvw_ref[:, KV_WIDTH:].astype(F32).T.astype(BF16)
    for hh in range(N_KV_HEADS):
        vt_ref[hh * VT_ROWS:hh * VT_ROWS + HEAD_DIM] = v_t[hh * HEAD_DIM:(hh + 1) * HEAD_DIM]
        vt_ref[hh * VT_ROWS + HEAD_DIM:(hh + 1) * VT_ROWS] = jnp.ones((VT_ROWS - HEAD_DIM, tq + 2 * BLOCK), BF16)

    nt_dims = (((1,), (1,)), ((), ()))
    first_half = lax.broadcasted_iota(jnp.int32, (1, 2 * BLOCK), 1) < BLOCK

    def sink_lanes(hh, par):
        return jnp.where(first_half, sink_ref[hh * GROUP + par], sink_ref[hh * GROUP + 2 + par])

    def scores(j, slot):
        r0 = j * BLOCK
        gb = tile * blocks_per_tile + j
        variant = jnp.where(gb == 0, 0, jnp.where(gb == n_blocks - 1, 2, 1))
        for hh in range(N_KV_HEADS):
            qp = q_ref[0, pl.ds(r0, BLOCK), hh * 4 * HEAD_DIM:(hh + 1) * 4 * HEAD_DIM]
            qs = jnp.concatenate([qp[:, :2 * HEAD_DIM], qp[:, 2 * HEAD_DIM:]], axis=0)
            for par in range(2):
                u = 2 * hh + par
                kz = kz_ref[pl.ds(r0, 3 * BLOCK), u * 2 * HEAD_DIM:(u + 1) * 2 * HEAD_DIM]
                bias = jnp.concatenate([bias_ref[variant, hh * GROUP + par],
                                        bias_ref[variant, hh * GROUP + 2 + par]], axis=1)
                s = lax.dot_general(kz, qs, nt_dims, preferred_element_type=F32) + bias
                s_ref[slot, u] = s
                m = jnp.maximum(jnp.max(s, axis=0, keepdims=True), sink_lanes(hh, par))
                m_ref[slot, u] = jnp.broadcast_to(m, (F32_SUBLANES, 2 * BLOCK))

    def finish(j, slot):
        r0 = j * BLOCK
        for hh in range(N_KV_HEADS):
            vt = vt_ref[hh * VT_ROWS:(hh + 1) * VT_ROWS, pl.ds(r0, 3 * BLOCK)]
            outs = []
            for par in range(2):
                u = 2 * hh + par
                m = m_ref[slot, u][0:1]
                p = jnp.exp2(s_ref[slot, u] - m)
                o_t = jnp.dot(vt, p.astype(BF16), preferred_element_type=F32)
                l = o_t[HEAD_DIM:HEAD_DIM + 1] + jnp.exp2(sink_lanes(hh, par) - m)
                outs.append(o_t[:HEAD_DIM] * (1.0 / l))
            for t in range(2):
                pair_t = jnp.concatenate([outs[0][:, t * BLOCK:(t + 1) * BLOCK],
                                          outs[1][:, t * BLOCK:(t + 1) * BLOCK]], axis=0)
                pair = 2 * hh + t
                o_ref[pl.ds(r0, BLOCK), pair * 2 * HEAD_DIM:(pair + 1) * 2 * HEAD_DIM] = pair_t.T

    def project(r0, rows):
        og = (o_ref[r0:r0 + rows] * sg_ref[r0:r0 + rows]).astype(BF16)
        x1 = x_ref[0, r0:r0 + rows] + jnp.dot(og, wo_ref[...], preferred_element_type=F32)
        out_ref[0, r0:r0 + rows] = x1
        hn_ref[0, r0:r0 + rows] = _rms_scale(x1, gf_ref[...]).astype(BF16)

    def gate(r0, rows):
        sg_ref[r0:r0 + rows] = _silu(jnp.dot(h_ref[0, r0:r0 + rows], wg_ref[...], preferred_element_type=F32))

    scores(0, 0)
    for j in range(blocks_per_tile):
        if j + 1 < blocks_per_tile:
            scores(j + 1, (j + 1) % 2)
        if j % PROJ_BLOCKS == 0:
            gate(j * BLOCK, PROJ_BLOCKS * BLOCK)
        finish(j, j % 2)
        if (j + 1) % PROJ_BLOCKS == 0:
            project((j + 1 - PROJ_BLOCKS) * BLOCK, PROJ_BLOCKS * BLOCK)


def _attention(sink, q, kv, h, x, bias, wg, wo, gf):
    b, s, _ = x.shape
    tq = ATTN_ROWS
    r = tq // BLOCK
    last_block = s // BLOCK - 1
    tile3 = lambda bi, i: (bi, i, 0)
    vec = pl.BlockSpec((1, D_MODEL), lambda bi, i: (0, 0))
    weight = pl.BlockSpec((D_MODEL, D_MODEL), lambda bi, i: (0, 0), pipeline_mode=pl.Buffered(1))
    return pl.pallas_call(
        _attn_kernel,
        grid=(b, s // tq),
        in_specs=[
            pl.BlockSpec(memory_space=pltpu.SMEM),
            pl.BlockSpec((1, tq, D_MODEL), tile3),
            pl.BlockSpec((1, tq, D_MODEL), tile3),
            pl.BlockSpec((1, tq, D_MODEL), tile3),
            pl.BlockSpec((1, BLOCK, 2 * KV_WIDTH), lambda bi, i: (bi, jnp.maximum(i * r - 1, 0), 0)),
            pl.BlockSpec((1, tq, 2 * KV_WIDTH), tile3),
            pl.BlockSpec((1, BLOCK, 2 * KV_WIDTH), lambda bi, i: (bi, jnp.minimum((i + 1) * r, last_block), 0)),
            pl.BlockSpec((3, N_HEADS, 3 * BLOCK, BLOCK), lambda bi, i: (0, 0, 0, 0),
                         pipeline_mode=pl.Buffered(1)),
            weight,
            weight,
            vec,
        ],
        out_specs=[pl.BlockSpec((1, tq, D_MODEL), tile3), pl.BlockSpec((1, tq, D_MODEL), tile3)],
        out_shape=[jax.ShapeDtypeStruct((b, s, D_MODEL), F32), jax.ShapeDtypeStruct((b, s, D_MODEL), BF16)],
        scratch_shapes=[
            pltpu.VMEM((tq + 2 * BLOCK, 2 * KV_WIDTH), BF16),
            pltpu.VMEM((tq + 2 * BLOCK, 2 * N_KV_HEADS * 2 * HEAD_DIM), BF16),
            pltpu.VMEM((N_KV_HEADS * VT_ROWS, tq + 2 * BLOCK), BF16),
            pltpu.VMEM((2, N_HEADS // 2, 3 * BLOCK, 2 * BLOCK), F32),
            pltpu.VMEM((2, N_HEADS // 2, F32_SUBLANES, 2 * BLOCK), F32),
            pltpu.VMEM((tq, D_MODEL), F32),
            pltpu.VMEM((tq, D_MODEL), F32),
        ],
        compiler_params=_params("parallel", "parallel"),
        name="attention",
    )(sink, q, h, x, kv, kv, kv, bias, wg, wo, gf)


def _stage1_kernel(w_ref, cs_ref, h_ref, y_ref):
    _, n1, tn2, d = h_ref.shape
    hs = jnp.swapaxes(h_ref[0], 0, 1).reshape(tn2 * n1, d)

    def channel_dft(grp):
        lo, hi = grp * FGROUP_CH, (grp + 1) * FGROUP_CH
        return jnp.dot(hs[:, lo:hi], cs_ref[...], preferred_element_type=F32).astype(BF16)

    t_next = channel_dft(0)
    for grp in range(N_FGROUPS):
        t = t_next
        ys = []
        for j in range(tn2):
            tj = t[j * n1:(j + 1) * n1]
            u = jnp.concatenate([tj[:, :FGROUP_CH], tj[:, FGROUP_CH:]], axis=0)
            ys.append(jnp.dot(w_ref[...], u, preferred_element_type=F32).astype(BF16))
        if grp + 1 < N_FGROUPS:
            t_next = channel_dft(grp + 1)
        y = jnp.swapaxes(jnp.stack(ys, axis=0), 0, 1)
        y_ref[0, :, :, :, grp * FGROUP_CH:(grp + 1) * FGROUP_CH] = y.reshape(2, n1, tn2, FGROUP_CH)


def _dft_stage1(w1, cs, h4):
    b, n1, n2, d = h4.shape
    const = lambda bi, i: (0, 0)
    return pl.pallas_call(
        _stage1_kernel,
        grid=(b, n2 // BF16_SUBLANES),
        in_specs=[
            pl.BlockSpec((2 * n1, 2 * n1), const),
            pl.BlockSpec((FGROUP_CH, 2 * FGROUP_CH), const),
            pl.BlockSpec((1, n1, BF16_SUBLANES, d), lambda bi, i: (bi, 0, i, 0)),
        ],
        out_specs=pl.BlockSpec((1, 2, n1, BF16_SUBLANES, d), lambda bi, i: (bi, 0, 0, i, 0)),
        out_shape=jax.ShapeDtypeStruct((b, 2, n1, n2, d), BF16),
        compiler_params=_params("parallel", "parallel"),
        name="dft_stage1",
    )(w1, cs, h4)


def _stage2_kernel(y_ref, m_ref, x_ref, gf_ref, wg_ref, wo_ref, gl_ref, out_ref, f_ref):
    tk1, n2 = m_ref.shape[0], m_ref.shape[1]
    fs = []
    for j in range(tk1):
        ycat = jnp.concatenate([y_ref[0, 0, j], y_ref[0, 1, j]], axis=0)
        fs.append(jnp.dot(m_ref[j], ycat, preferred_element_type=F32))
    f_ref[...] = jnp.swapaxes(jnp.stack(fs, axis=0), 0, 1)
    ck2 = STAGE2_CHUNK // tk1
    for c0 in range(0, n2, ck2):
        x = x_ref[0, c0:c0 + ck2].reshape(STAGE2_CHUNK, D_MODEL)
        h = _rms_scale(x, gf_ref[...]).astype(BF16)
        sg = _silu(jnp.dot(h, wg_ref[...], preferred_element_type=F32))
        fg = (f_ref[c0:c0 + ck2].reshape(STAGE2_CHUNK, D_MODEL) * sg).astype(BF16)
        xr = x + jnp.dot(fg, wo_ref[...], preferred_element_type=F32)
        out_ref[0, c0:c0 + ck2] = _rms_scale(xr, gl_ref[...]).reshape(ck2, tk1, D_MODEL)


def _dft_stage2(y5, mtab, x4, gf, wg, wo, gl):
    b, _, n1, n2, d = y5.shape
    tk1 = STAGE2_ROWS // n2
    rows = pl.BlockSpec((1, n2, tk1, d), lambda bi, i: (bi, 0, i, 0))
    const = lambda bi, i: (0, 0)
    return pl.pallas_call(
        _stage2_kernel,
        grid=(b, n1 // tk1),
        in_specs=[
            pl.BlockSpec((1, 2, tk1, n2, d), lambda bi, i: (bi, 0, i, 0, 0)),
            pl.BlockSpec((tk1, n2, 2 * n2), lambda bi, i: (i, 0, 0)),
            rows,
            pl.BlockSpec((1, d), const),
            pl.BlockSpec((d, d), const),
            pl.BlockSpec((d, d), const),
            pl.BlockSpec((1, d), const),
        ],
        out_specs=rows,
        out_shape=jax.ShapeDtypeStruct(x4.shape, F32),
        scratch_shapes=[pltpu.VMEM((n2, tk1, d), F32)],
        compiler_params=_params("parallel", "parallel"),
        name="dft_stage2",
    )(y5, mtab, x4, gf, wg, wo, gl)


def _trunk(x, bias, attn_norm, wq, wkv, wgate_a, wo_a, sink, fourier_norm, wgate_f, wo_f, final_norm, cs):
    b, s, d = x.shape
    n2 = s // DFT_N1
    q, kv, h = _inproj(x.reshape(b * s, d), attn_norm, wq, wkv)
    x1, hn = _attention(sink, q.reshape(b, s, d), kv.reshape(b, s, 2 * KV_WIDTH), h.reshape(b, s, d), x, bias,
                        wgate_a, wo_a, fourier_norm)
    w1 = jnp.asarray(_stage1_table(s)).astype(BF16)
    mtab = jnp.asarray(_stage2_table(s)).astype(BF16)
    y5 = _dft_stage1(w1, cs, hn.reshape(b, DFT_N1, n2, d))
    out = _dft_stage2(y5, mtab, x1.reshape(b, n2, DFT_N1, d), fourier_norm, wgate_f, wo_f, final_norm)
    return out.reshape(b, s, d)


def kernel(x_prompt, x_sample, rel_bias, attn_norm, attn_w_in, attn_w_out, attn_sink,
           fourier_norm, fourier_w_gate, fourier_w_out, final_norm):
    bucket, band_mask = _band_tables()
    onehot = (jnp.asarray(bucket.T.reshape(-1))[:, None] == jnp.arange(NUM_BUCKETS)[None, :]).astype(F32)
    bias = jnp.einsum("pb,bh->hp", onehot, rel_bias.astype(F32), precision=lax.Precision.HIGHEST)
    bias = bias.reshape(N_HEADS, 3 * BLOCK, BLOCK) * LOG2E + jnp.asarray(band_mask.T)[None]
    key = jnp.arange(3 * BLOCK)[None, :, None]
    bias = jnp.stack([jnp.where(key < BLOCK, MASK_VALUE, bias), bias,
                      jnp.where(key >= 2 * BLOCK, MASK_VALUE, bias)], axis=0)
    w_in = attn_w_in[0]
    wq = w_in[:, :D_MODEL].astype(BF16)
    wkv = w_in[:, D_MODEL:D_MODEL + 2 * KV_WIDTH].astype(BF16)
    wgate_a = w_in[:, D_MODEL + 2 * KV_WIDTH:].astype(BF16)
    cs = jnp.asarray(_channel_dft_table()).astype(BF16)
    args = (bias, attn_norm[0][None], wq, wkv, wgate_a, attn_w_out[0].astype(BF16), attn_sink[0] * LOG2E,
            fourier_norm[0][None], fourier_w_gate[0].astype(BF16), fourier_w_out[0].astype(BF16),
            final_norm[None], cs)
    return (_trunk(x_prompt, *args), _trunk(x_sample, *args))
```

```python
import functools
import math

import jax
import jax.numpy as jnp
import numpy as np
from jax import lax
from jax.experimental import pallas as pl
from jax.experimental.pallas import tpu as pltpu

D_MODEL = 1024
HEAD_DIM = 64
N_HEADS = 16
N_KV_HEADS = 4
GROUP = 4
KV_WIDTH = N_KV_HEADS * HEAD_DIM
WINDOW = 128
BLOCK = 128
NUM_BUCKETS = 32
MAX_DISTANCE = 128
N_FGROUPS = 4
FGROUP_CH = 256
RMS_EPS = 1e-6
MASK_VALUE = -1e30
LOG2E = math.log2(math.e)

DFT_N1 = 128
ROW_TILE = 1024
ATTN_ROWS = 1024
VT_ROWS = HEAD_DIM + 16
PROJ_BLOCKS = 2
BF16_SUBLANES = 16
F32_SUBLANES = 8
STAGE2_ROWS = 1024
STAGE2_CHUNK = 512
VMEM_LIMIT = 48 * 1024 * 1024
ATTN_VMEM_LIMIT = 58 * 1024 * 1024

BF16 = jnp.bfloat16
F32 = jnp.float32


def _params(*sem, vmem=VMEM_LIMIT):
    return pltpu.CompilerParams(dimension_semantics=sem, vmem_limit_bytes=vmem)


def _rms_scale(x, g):
    inv = lax.rsqrt(jnp.mean(x * x, axis=-1, keepdims=True) + RMS_EPS)
    return x * inv * g


def _silu(z):
    hz = 0.5 * z
    return hz + hz * jnp.tanh(hz)


def _t5_bucket_np(rel):
    half = NUM_BUCKETS // 2
    n = -rel
    ret = (n < 0).astype(np.int32) * half
    n = np.abs(n)
    max_exact = half // 2
    is_small = n < max_exact
    large = max_exact + (np.log(np.maximum(n, 1) / max_exact) / math.log(MAX_DISTANCE / max_exact)
                         * (half - max_exact)).astype(np.int32)
    large = np.minimum(large, half - 1)
    return (ret + np.where(is_small, n, large)).astype(np.int32)


@functools.lru_cache(maxsize=None)
def _band_tables():
    qi = np.arange(BLOCK)[:, None]
    kj = np.arange(3 * BLOCK)[None, :]
    rel = kj - BLOCK - qi
    band = np.abs(rel) <= WINDOW
    return _t5_bucket_np(rel), np.where(band, 0.0, MASK_VALUE).astype(np.float32)


@functools.lru_cache(maxsize=None)
def _channel_dft_table():
    c = np.arange(FGROUP_CH)
    ang = 2.0 * np.pi * ((c[:, None] * c[None, :]) % FGROUP_CH) / FGROUP_CH
    return (np.concatenate([np.cos(ang), np.sin(ang)], axis=1) / 16.0).astype(np.float32)


@functools.lru_cache(maxsize=None)
def _stage1_table(seq):
    k = np.arange(DFT_N1)
    ang = 2.0 * np.pi * ((k[:, None] * k[None, :]) % DFT_N1) / DFT_N1
    c, s = np.cos(ang), np.sin(ang)
    w = np.block([[c, -s], [-s, -c]]) / math.sqrt(seq)
    return w.astype(np.float32)


@functools.lru_cache(maxsize=None)
def _stage2_table(seq):
    n2_len = seq // DFT_N1
    k1 = np.arange(DFT_N1)[:, None, None]
    k2 = np.arange(n2_len)[None, :, None]
    n2 = np.arange(n2_len)[None, None, :]
    ang = 2.0 * np.pi * (((k1 + DFT_N1 * k2) * n2) % seq) / seq
    return np.concatenate([np.cos(ang), np.sin(ang)], axis=2).astype(np.float32)


def _inproj_kernel(x_ref, g_ref, wq_ref, wkv_ref, q_ref, kv_ref):
    h = _rms_scale(x_ref[...], g_ref[...]).astype(BF16)
    q = jnp.dot(h, wq_ref[...], preferred_element_type=F32)
    q_ref[...] = (q * (HEAD_DIM ** -0.5 * LOG2E)).astype(BF16)
    kv_ref[...] = jnp.dot(h, wkv_ref[...], preferred_element_type=F32).astype(BF16)


def _inproj(x2, g, wq, wkv):
    t = x2.shape[0]
    row = lambda i: (i, 0)
    const = lambda i: (0, 0)
    return pl.pallas_call(
        _inproj_kernel,
        grid=(t // ROW_TILE,),
        in_specs=[
            pl.BlockSpec((ROW_TILE, D_MODEL), row),
            pl.BlockSpec((1, D_MODEL), const),
            pl.BlockSpec((D_MODEL, D_MODEL), const),
            pl.BlockSpec((D_MODEL, 2 * KV_WIDTH), const),
        ],
        out_specs=[
            pl.BlockSpec((ROW_TILE, D_MODEL), row),
            pl.BlockSpec((ROW_TILE, 2 * KV_WIDTH), row),
        ],
        out_shape=[
            jax.ShapeDtypeStruct((t, D_MODEL), BF16),
            jax.ShapeDtypeStruct((t, 2 * KV_WIDTH), BF16),
        ],
        compiler_params=_params("parallel"),
        name="inproj",
    )(x2, g, wq, wkv)


def _attn_kernel(sink_ref, q_ref, x_ref, kvp_ref, kvc_ref, kvn_ref, bias_ref, ga_ref, wg_ref, wo_ref, gf_ref,
                 out_ref, hn_ref, kvw_ref, kz_ref, vt_ref, s_ref, m_ref, o_ref, sg_ref):
    tq = q_ref.shape[1]
    blocks_per_tile = tq // BLOCK
    tile = pl.program_id(1)

    kvw_ref[0:BLOCK, :] = kvp_ref[0]
    kvw_ref[BLOCK:BLOCK + tq, :] = kvc_ref[0]
    kvw_ref[BLOCK + tq:, :] = kvn_ref[0]

    low = lax.broadcasted_iota(jnp.int32, (1, 2 * HEAD_DIM), 1) < HEAD_DIM
    for c in range(N_KV_HEADS // 2):
        kc = kvw_ref[:, 2 * c * HEAD_DIM:(2 * c + 2) * HEAD_DIM]
        kc_sw = jnp.concatenate([kc[:, HEAD_DIM:], kc[:, :HEAD_DIM]], axis=1)
        zero = jnp.zeros_like(kc)
        for t, tab in enumerate((jnp.where(low, kc, zero), jnp.where(low, zero, kc_sw),
                                 jnp.where(low, kc_sw, zero), jnp.where(low, zero, kc))):
            kz_ref[:, (4 * c + t) * 2 * HEAD_DIM:(4 * c + t + 1) * 2 * HEAD_DIM] = tab
    v_t = kvw_ref[:, KV_WIDTH:].astype(F32).T.astype(BF16)
    for hh in range(N_KV_HEADS):
        vt_ref[hh * VT_ROWS:hh * VT_ROWS + HEAD_DIM] = v_t[hh * HEAD_DIM:(hh + 1) * HEAD_DIM]
        vt_ref[hh * VT_ROWS + HEAD_DIM:(hh + 1) * VT_ROWS] = jnp.ones((VT_ROWS - HEAD_DIM, tq + 2 * BLOCK), BF16)

    nt_dims = (((1,), (1,)), ((), ()))
    first_half = lax.broadcasted_iota(jnp.int32, (1, 2 * BLOCK), 1) < BLOCK

    def sink_lanes(hh, par):
        return jnp.where(first_half, sink_ref[hh * GROUP + par], sink_ref[hh * GROUP + 2 + par])

    def scores(j, hh):
        slot, r0 = j % 2, j * BLOCK
        qp = q_ref[0, pl.ds(r0, BLOCK), hh * 4 * HEAD_DIM:(hh + 1) * 4 * HEAD_DIM]
        qs = jnp.concatenate([qp[:, :2 * HEAD_DIM], qp[:, 2 * HEAD_DIM:]], axis=0)
        for par in range(2):
            u = 2 * hh + par
            kz = kz_ref[pl.ds(r0, 3 * BLOCK), u * 2 * HEAD_DIM:(u + 1) * 2 * HEAD_DIM]
            bias = jnp.concatenate([bias_ref[hh * GROUP + par], bias_ref[hh * GROUP + 2 + par]], axis=1)
            s = lax.dot_general(kz, qs, nt_dims, preferred_element_type=F32) + bias
            if j == 0:
                s = jnp.concatenate([s[:BLOCK] + jnp.where(tile == 0, MASK_VALUE, 0.0), s[BLOCK:]], axis=0)
            if j == blocks_per_tile - 1:
                last = jnp.where(tile == pl.num_programs(1) - 1, MASK_VALUE, 0.0)
                s = jnp.concatenate([s[:2 * BLOCK], s[2 * BLOCK:] + last], axis=0)
            s_ref[slot, u] = s
            m = jnp.maximum(jnp.max(s, axis=0, keepdims=True), sink_lanes(hh, par))
            m_ref[slot, u] = jnp.broadcast_to(m, (F32_SUBLANES, 2 * BLOCK))

    def finish(j, hh):
        slot, r0 = j % 2, j * BLOCK
        vt = vt_ref[hh * VT_ROWS:(hh + 1) * VT_ROWS, pl.ds(r0, 3 * BLOCK)]
        outs = []
        for par in range(2):
            u = 2 * hh + par
            m = m_ref[slot, u][0:1]
            p = jnp.exp2(s_ref[slot, u] - m)
            o_t = jnp.dot(vt, p.astype(BF16), preferred_element_type=F32)
            l = o_t[HEAD_DIM:HEAD_DIM + 1] + jnp.exp2(sink_lanes(hh, par) - m)
            outs.append(o_t[:HEAD_DIM] * (1.0 / l))
        for t in range(2):
            pair_t = jnp.concatenate([outs[0][:, t * BLOCK:(t + 1) * BLOCK],
                                      outs[1][:, t * BLOCK:(t + 1) * BLOCK]], axis=0)
            pair = 2 * hh + t
            o_ref[pl.ds(r0, BLOCK), pair * 2 * HEAD_DIM:(pair + 1) * 2 * HEAD_DIM] = pair_t.T

    def project(r0, rows):
        og = (o_ref[r0:r0 + rows] * sg_ref[r0:r0 + rows]).astype(BF16)
        x1 = x_ref[0, r0:r0 + rows] + jnp.dot(og, wo_ref[...], preferred_element_type=F32)
        out_ref[0, r0:r0 + rows] = x1
        hn_ref[0, r0:r0 + rows] = _rms_scale(x1, gf_ref[...]).astype(BF16)

    def gate(r0, rows):
        h = _rms_scale(x_ref[0, r0:r0 + rows], ga_ref[...]).astype(BF16)
        sg_ref[r0:r0 + rows] = _silu(jnp.dot(h, wg_ref[...], preferred_element_type=F32))

    for hh in range(N_KV_HEADS):
        scores(0, hh)
    for j in range(blocks_per_tile):
        if j % PROJ_BLOCKS == 0:
            gate(j * BLOCK, PROJ_BLOCKS * BLOCK)
        for hh in range(N_KV_HEADS):
            if j + 1 < blocks_per_tile:
                scores(j + 1, hh)
            finish(j, hh)
        if (j + 1) % PROJ_BLOCKS == 0:
            project((j + 1 - PROJ_BLOCKS) * BLOCK, PROJ_BLOCKS * BLOCK)


def _attention(sink, q, kv, x, bias, ga, wg, wo, gf):
    b, s, _ = x.shape
    tq = ATTN_ROWS
    r = tq // BLOCK
    last_block = s // BLOCK - 1
    tile3 = lambda bi, i: (bi, i, 0)
    vec = pl.BlockSpec((1, D_MODEL), lambda bi, i: (0, 0))
    weight = pl.BlockSpec((D_MODEL, D_MODEL), lambda bi, i: (0, 0), pipeline_mode=pl.Buffered(1))
    return pl.pallas_call(
        _attn_kernel,
        grid=(b, s // tq),
        in_specs=[
            pl.BlockSpec(memory_space=pltpu.SMEM),
            pl.BlockSpec((1, tq, D_MODEL), tile3),
            pl.BlockSpec((1, tq, D_MODEL), tile3),
            pl.BlockSpec((1, BLOCK, 2 * KV_WIDTH), lambda bi, i: (bi, jnp.maximum(i * r - 1, 0), 0)),
            pl.BlockSpec((1, tq, 2 * KV_WIDTH), tile3),
            pl.BlockSpec((1, BLOCK, 2 * KV_WIDTH), lambda bi, i: (bi, jnp.minimum((i + 1) * r, last_block), 0)),
            pl.BlockSpec((N_HEADS, 3 * BLOCK, BLOCK), lambda bi, i: (0, 0, 0), pipeline_mode=pl.Buffered(1)),
            vec,
            weight,
            weight,
            vec,
        ],
        out_specs=[pl.BlockSpec((1, tq, D_MODEL), tile3), pl.BlockSpec((1, tq, D_MODEL), tile3)],
        out_shape=[jax.ShapeDtypeStruct((b, s, D_MODEL), F32), jax.ShapeDtypeStruct((b, s, D_MODEL), BF16)],
        scratch_shapes=[
            pltpu.VMEM((tq + 2 * BLOCK, 2 * KV_WIDTH), BF16),
            pltpu.VMEM((tq + 2 * BLOCK, 2 * N_KV_HEADS * 2 * HEAD_DIM), BF16),
            pltpu.VMEM((N_KV_HEADS * VT_ROWS, tq + 2 * BLOCK), BF16),
            pltpu.VMEM((2, N_HEADS // 2, 3 * BLOCK, 2 * BLOCK), F32),
            pltpu.VMEM((2, N_HEADS // 2, F32_SUBLANES, 2 * BLOCK), F32),
            pltpu.VMEM((tq, D_MODEL), F32),
            pltpu.VMEM((tq, D_MODEL), F32),
        ],
        compiler_params=_params("parallel", "parallel", vmem=ATTN_VMEM_LIMIT),
        name="attention",
    )(sink, q, x, kv, kv, kv, bias, ga, wg, wo, gf)


def _stage1_kernel(w_ref, cs_ref, h_ref, y_ref):
    _, n1, tn2, d = h_ref.shape
    hs = jnp.swapaxes(h_ref[0], 0, 1).reshape(tn2 * n1, d)

    def channel_dft(grp):
        lo, hi = grp * FGROUP_CH, (grp + 1) * FGROUP_CH
        return jnp.dot(hs[:, lo:hi], cs_ref[...], preferred_element_type=F32).astype(BF16)

    t_next = channel_dft(0)
    for grp in range(N_FGROUPS):
        t = t_next
        ys = []
        for j in range(tn2):
            tj = t[j * n1:(j + 1) * n1]
            u = jnp.concatenate([tj[:, :FGROUP_CH], tj[:, FGROUP_CH:]], axis=0)
            ys.append(jnp.dot(w_ref[...], u, preferred_element_type=F32).astype(BF16))
        if grp + 1 < N_FGROUPS:
            t_next = channel_dft(grp + 1)
        y = jnp.swapaxes(jnp.stack(ys, axis=0), 0, 1)
        y_ref[0, :, :, :, grp * FGROUP_CH:(grp + 1) * FGROUP_CH] = y.reshape(2, n1, tn2, FGROUP_CH)


def _dft_stage1(w1, cs, h4):
    b, n1, n2, d = h4.shape
    const = lambda bi, i: (0, 0)
    return pl.pallas_call(
        _stage1_kernel,
        grid=(b, n2 // BF16_SUBLANES),
        in_specs=[
            pl.BlockSpec((2 * n1, 2 * n1), const),
            pl.BlockSpec((FGROUP_CH, 2 * FGROUP_CH), const),
            pl.BlockSpec((1, n1, BF16_SUBLANES, d), lambda bi, i: (bi, 0, i, 0)),
        ],
        out_specs=pl.BlockSpec((1, 2, n1, BF16_SUBLANES, d), lambda bi, i: (bi, 0, 0, i, 0)),
        out_shape=jax.ShapeDtypeStruct((b, 2, n1, n2, d), BF16),
        compiler_params=_params("parallel", "parallel"),
        name="dft_stage1",
    )(w1, cs, h4)


def _stage2_kernel(y_ref, m_ref, x_ref, gf_ref, wg_ref, wo_ref, gl_ref, out_ref, f_ref):
    tk1, n2 = m_ref.shape[0], m_ref.shape[1]
    fs = []
    for j in range(tk1):
        ycat = jnp.concatenate([y_ref[0, 0, j], y_ref[0, 1, j]], axis=0)
        fs.append(jnp.dot(m_ref[j], ycat, preferred_element_type=F32))
    f_ref[...] = jnp.swapaxes(jnp.stack(fs, axis=0), 0, 1)
    ck2 = STAGE2_CHUNK // tk1
    for c0 in range(0, n2, ck2):
        x = x_ref[0, c0:c0 + ck2].reshape(STAGE2_CHUNK, D_MODEL)
        h = _rms_scale(x, gf_ref[...]).astype(BF16)
        sg = _silu(jnp.dot(h, wg_ref[...], preferred_element_type=F32))
        fg = (f_ref[c0:c0 + ck2].reshape(STAGE2_CHUNK, D_MODEL) * sg).astype(BF16)
        xr = x + jnp.dot(fg, wo_ref[...], preferred_element_type=F32)
        out_ref[0, c0:c0 + ck2] = _rms_scale(xr, gl_ref[...]).reshape(ck2, tk1, D_MODEL)


def _dft_stage2(y5, mtab, x4, gf, wg, wo, gl):
    b, _, n1, n2, d = y5.shape
    tk1 = STAGE2_ROWS // n2
    rows = pl.BlockSpec((1, n2, tk1, d), lambda bi, i: (bi, 0, i, 0))
    const = lambda bi, i: (0, 0)
    return pl.pallas_call(
        _stage2_kernel,
        grid=(b, n1 // tk1),
        in_specs=[
            pl.BlockSpec((1, 2, tk1, n2, d), lambda bi, i: (bi, 0, i, 0, 0)),
            pl.BlockSpec((tk1, n2, 2 * n2), lambda bi, i: (i, 0, 0)),
            rows,
            pl.BlockSpec((1, d), const),
            pl.BlockSpec((d, d), const),
            pl.BlockSpec((d, d), const),
            pl.BlockSpec((1, d), const),
        ],
        out_specs=rows,
        out_shape=jax.ShapeDtypeStruct(x4.shape, F32),
        scratch_shapes=[pltpu.VMEM((n2, tk1, d), F32)],
        compiler_params=_params("parallel", "parallel"),
        name="dft_stage2",
    )(y5, mtab, x4, gf, wg, wo, gl)


def _trunk(x, bias, attn_norm, wq, wkv, wgate_a, wo_a, sink, fourier_norm, wgate_f, wo_f, final_norm, cs):
    b, s, d = x.shape
    n2 = s // DFT_N1
    q, kv = _inproj(x.reshape(b * s, d), attn_norm, wq, wkv)
    x1, hn = _attention(sink, q.reshape(b, s, d), kv.reshape(b, s, 2 * KV_WIDTH), x, bias,
                        attn_norm, wgate_a, wo_a, fourier_norm)
    w1 = jnp.asarray(_stage1_table(s)).astype(BF16)
    mtab = jnp.asarray(_stage2_table(s)).astype(BF16)
    y5 = _dft_stage1(w1, cs, hn.reshape(b, DFT_N1, n2, d))
    out = _dft_stage2(y5, mtab, x1.reshape(b, n2, DFT_N1, d), fourier_norm, wgate_f, wo_f, final_norm)
    return out.reshape(b, s, d)


def kernel(x_prompt, x_sample, rel_bias, attn_norm, attn_w_in, attn_w_out, attn_sink,
           fourier_norm, fourier_w_gate, fourier_w_out, final_norm):
    bucket, band_mask = _band_tables()
    onehot = (jnp.asarray(bucket.T.reshape(-1))[:, None] == jnp.arange(NUM_BUCKETS)[None, :]).astype(F32)
    bias = jnp.einsum("pb,bh->hp", onehot, rel_bias.astype(F32), precision=lax.Precision.HIGHEST)
    bias = bias.reshape(N_HEADS, 3 * BLOCK, BLOCK) * LOG2E + jnp.asarray(band_mask.T)[None]
    w_in = attn_w_in[0]
    wq = w_in[:, :D_MODEL].astype(BF16)
    wkv = w_in[:, D_MODEL:D_MODEL + 2 * KV_WIDTH].astype(BF16)
    wgate_a = w_in[:, D_MODEL + 2 * KV_WIDTH:].astype(BF16)
    cs = jnp.asarray(_channel_dft_table()).astype(BF16)
    args = (bias, attn_norm[0][None], wq, wkv, wgate_a, attn_w_out[0].astype(BF16), attn_sink[0] * LOG2E,
            fourier_norm[0][None], fourier_w_gate[0].astype(BF16), fourier_w_out[0].astype(BF16),
            final_norm[None], cs)
    return (_trunk(x_prompt, *args), _trunk(x_sample, *args))
```

```python
import functools
import math

import jax
import jax.numpy as jnp
import numpy as np
from jax import lax
from jax.experimental import pallas as pl
from jax.experimental.pallas import tpu as pltpu

D_MODEL = 1024
HEAD_DIM = 64
N_HEADS = 16
N_KV_HEADS = 4
GROUP = 4
KV_WIDTH = N_KV_HEADS * HEAD_DIM
WINDOW = 128
BLOCK = 128
NUM_BUCKETS = 32
MAX_DISTANCE = 128
N_FGROUPS = 4
FGROUP_CH = 256
RMS_EPS = 1e-6
MASK_VALUE = -1e30
LOG2E = math.log2(math.e)

DFT_N1 = 128
ROW_TILE = 2048
ATTN_ROWS = 1024
VT_ROWS = HEAD_DIM + 16
PROJ_BLOCKS = 2
BF16_SUBLANES = 16
F32_SUBLANES = 8
STAGE2_ROWS = 1024
STAGE2_CHUNK = 512
VMEM_LIMIT = 48 * 1024 * 1024
ATTN_VMEM_LIMIT = 58 * 1024 * 1024

BF16 = jnp.bfloat16
F32 = jnp.float32


def _params(*sem, vmem=VMEM_LIMIT):
    return pltpu.CompilerParams(dimension_semantics=sem, vmem_limit_bytes=vmem)


def _rms_scale(x, g):
    inv = lax.rsqrt(jnp.mean(x * x, axis=-1, keepdims=True) + RMS_EPS)
    return x * inv * g


def _silu(z):
    hz = 0.5 * z
    return hz + hz * jnp.tanh(hz)


def _t5_bucket_np(rel):
    half = NUM_BUCKETS // 2
    n = -rel
    ret = (n < 0).astype(np.int32) * half
    n = np.abs(n)
    max_exact = half // 2
    is_small = n < max_exact
    large = max_exact + (np.log(np.maximum(n, 1) / max_exact) / math.log(MAX_DISTANCE / max_exact)
                         * (half - max_exact)).astype(np.int32)
    large = np.minimum(large, half - 1)
    return (ret + np.where(is_small, n, large)).astype(np.int32)


@functools.lru_cache(maxsize=None)
def _band_tables():
    qi = np.arange(BLOCK)[:, None]
    kj = np.arange(3 * BLOCK)[None, :]
    rel = kj - BLOCK - qi
    band = np.abs(rel) <= WINDOW
    return _t5_bucket_np(rel), np.where(band, 0.0, MASK_VALUE).astype(np.float32)


@functools.lru_cache(maxsize=None)
def _channel_dft_table():
    c = np.arange(FGROUP_CH)
    ang = 2.0 * np.pi * ((c[:, None] * c[None, :]) % FGROUP_CH) / FGROUP_CH
    return (np.concatenate([np.cos(ang), np.sin(ang)], axis=1) / 16.0).astype(np.float32)


@functools.lru_cache(maxsize=None)
def _stage1_table(seq):
    k = np.arange(DFT_N1)
    ang = 2.0 * np.pi * ((k[:, None] * k[None, :]) % DFT_N1) / DFT_N1
    c, s = np.cos(ang), np.sin(ang)
    w = np.block([[c, -s], [-s, -c]]) / math.sqrt(seq)
    return w.astype(np.float32)


@functools.lru_cache(maxsize=None)
def _stage2_table(seq):
    n2_len = seq // DFT_N1
    k1 = np.arange(DFT_N1)[:, None, None]
    k2 = np.arange(n2_len)[None, :, None]
    n2 = np.arange(n2_len)[None, None, :]
    ang = 2.0 * np.pi * (((k1 + DFT_N1 * k2) * n2) % seq) / seq
    return np.concatenate([np.cos(ang), np.sin(ang)], axis=2).astype(np.float32)


def _inproj_kernel(x_ref, g_ref, wq_ref, wkv_ref, q_ref, kv_ref):
    h = _rms_scale(x_ref[...], g_ref[...]).astype(BF16)
    q = jnp.dot(h, wq_ref[...], preferred_element_type=F32)
    q_ref[...] = (q * (HEAD_DIM ** -0.5 * LOG2E)).astype(BF16)
    kv_ref[...] = jnp.dot(h, wkv_ref[...], preferred_element_type=F32).astype(BF16)


def _inproj(x2, g, wq, wkv):
    t = x2.shape[0]
    row = lambda i: (i, 0)
    const = lambda i: (0, 0)
    return pl.pallas_call(
        _inproj_kernel,
        grid=(t // ROW_TILE,),
        in_specs=[
            pl.BlockSpec((ROW_TILE, D_MODEL), row),
            pl.BlockSpec((1, D_MODEL), const),
            pl.BlockSpec((D_MODEL, D_MODEL), const),
            pl.BlockSpec((D_MODEL, 2 * KV_WIDTH), const),
        ],
        out_specs=[
            pl.BlockSpec((ROW_TILE, D_MODEL), row),
            pl.BlockSpec((ROW_TILE, 2 * KV_WIDTH), row),
        ],
        out_shape=[
            jax.ShapeDtypeStruct((t, D_MODEL), BF16),
            jax.ShapeDtypeStruct((t, 2 * KV_WIDTH), BF16),
        ],
        compiler_params=_params("parallel"),
        name="inproj",
    )(x2, g, wq, wkv)


def _attn_kernel(sink_ref, q_ref, x_ref, kvp_ref, kvc_ref, kvn_ref, bias_ref, ga_ref, wg_ref, wo_ref, gf_ref,
                 out_ref, hn_ref, kvw_ref, kz_ref, vt_ref, s_ref, m_ref, o_ref, sg_ref):
    tq = q_ref.shape[1]
    blocks_per_tile = tq // BLOCK
    tile = pl.program_id(1)

    kvw_ref[0:BLOCK, :] = kvp_ref[0]
    kvw_ref[BLOCK:BLOCK + tq, :] = kvc_ref[0]
    kvw_ref[BLOCK + tq:, :] = kvn_ref[0]

    low = lax.broadcasted_iota(jnp.int32, (1, 2 * HEAD_DIM), 1) < HEAD_DIM
    for c in range(N_KV_HEADS // 2):
        kc = kvw_ref[:, 2 * c * HEAD_DIM:(2 * c + 2) * HEAD_DIM]
        kc_sw = jnp.concatenate([kc[:, HEAD_DIM:], kc[:, :HEAD_DIM]], axis=1)
        zero = jnp.zeros_like(kc)
        for t, tab in enumerate((jnp.where(low, kc, zero), jnp.where(low, zero, kc_sw),
                                 jnp.where(low, kc_sw, zero), jnp.where(low, zero, kc))):
            kz_ref[:, (4 * c + t) * 2 * HEAD_DIM:(4 * c + t + 1) * 2 * HEAD_DIM] = tab
    v_t = kvw_ref[:, KV_WIDTH:].astype(F32).T.astype(BF16)
    for hh in range(N_KV_HEADS):
        vt_ref[hh * VT_ROWS:hh * VT_ROWS + HEAD_DIM] = v_t[hh * HEAD_DIM:(hh + 1) * HEAD_DIM]
        vt_ref[hh * VT_ROWS + HEAD_DIM:(hh + 1) * VT_ROWS] = jnp.ones((VT_ROWS - HEAD_DIM, tq + 2 * BLOCK), BF16)

    nt_dims = (((1,), (1,)), ((), ()))
    first_half = lax.broadcasted_iota(jnp.int32, (1, 2 * BLOCK), 1) < BLOCK

    def sink_lanes(hh, par):
        return jnp.where(first_half, sink_ref[hh * GROUP + par], sink_ref[hh * GROUP + 2 + par])

    def scores(j, hh):
        slot, r0 = j % 2, j * BLOCK
        qp = q_ref[0, pl.ds(r0, BLOCK), hh * 4 * HEAD_DIM:(hh + 1) * 4 * HEAD_DIM]
        qs = jnp.concatenate([qp[:, :2 * HEAD_DIM], qp[:, 2 * HEAD_DIM:]], axis=0)
        for par in range(2):
            u = 2 * hh + par
            kz = kz_ref[pl.ds(r0, 3 * BLOCK), u * 2 * HEAD_DIM:(u + 1) * 2 * HEAD_DIM]
            bias = jnp.concatenate([bias_ref[hh * GROUP + par], bias_ref[hh * GROUP + 2 + par]], axis=1)
            s = lax.dot_general(kz, qs, nt_dims, preferred_element_type=F32) + bias
            if j == 0:
                s = jnp.concatenate([s[:BLOCK] + jnp.where(tile == 0, MASK_VALUE, 0.0), s[BLOCK:]], axis=0)
            if j == blocks_per_tile - 1:
                last = jnp.where(tile == pl.num_programs(1) - 1, MASK_VALUE, 0.0)
                s = jnp.concatenate([s[:2 * BLOCK], s[2 * BLOCK:] + last], axis=0)
            s_ref[slot, u] = s
            m = jnp.maximum(jnp.max(s, axis=0, keepdims=True), sink_lanes(hh, par))
            m_ref[slot, u] = jnp.broadcast_to(m, (F32_SUBLANES, 2 * BLOCK))

    def finish(j, hh):
        slot, r0 = j % 2, j * BLOCK
        vt = vt_ref[hh * VT_ROWS:(hh + 1) * VT_ROWS, pl.ds(r0, 3 * BLOCK)]
        outs = []
        for par in range(2):
            u = 2 * hh + par
            m = m_ref[slot, u][0:1]
            p = jnp.exp2(s_ref[slot, u] - m)
            o_t = jnp.dot(vt, p.astype(BF16), preferred_element_type=F32)
            l = o_t[HEAD_DIM:HEAD_DIM + 1] + jnp.exp2(sink_lanes(hh, par) - m)
            outs.append(o_t[:HEAD_DIM] * (1.0 / l))
        for t in range(2):
            pair_t = jnp.concatenate([outs[0][:, t * BLOCK:(t + 1) * BLOCK],
                                      outs[1][:, t * BLOCK:(t + 1) * BLOCK]], axis=0)
            pair = 2 * hh + t
            o_ref[pl.ds(r0, BLOCK), pair * 2 * HEAD_DIM:(pair + 1) * 2 * HEAD_DIM] = pair_t.T

    def project(r0, rows):
        og = (o_ref[r0:r0 + rows] * sg_ref[r0:r0 + rows]).astype(BF16)
        x1 = x_ref[0, r0:r0 + rows] + jnp.dot(og, wo_ref[...], preferred_element_type=F32)
        out_ref[0, r0:r0 + rows] = x1
        hn_ref[0, r0:r0 + rows] = _rms_scale(x1, gf_ref[...]).astype(BF16)

    def gate(r0, rows):
        h = _rms_scale(x_ref[0, r0:r0 + rows], ga_ref[...]).astype(BF16)
        sg_ref[r0:r0 + rows] = _silu(jnp.dot(h, wg_ref[...], preferred_element_type=F32))

    for hh in range(N_KV_HEADS):
        scores(0, hh)
    for j in range(blocks_per_tile):
        if j % PROJ_BLOCKS == 0:
            gate(j * BLOCK, PROJ_BLOCKS * BLOCK)
        for hh in range(N_KV_HEADS):
            if j + 1 < blocks_per_tile:
                scores(j + 1, hh)
            finish(j, hh)
        if (j + 1) % PROJ_BLOCKS == 0:
            project((j + 1 - PROJ_BLOCKS) * BLOCK, PROJ_BLOCKS * BLOCK)


def _attention(sink, q, kv, x, bias, ga, wg, wo, gf):
    b, s, _ = x.shape
    tq = ATTN_ROWS
    r = tq // BLOCK
    last_block = s // BLOCK - 1
    tile3 = lambda bi, i: (bi, i, 0)
    vec = pl.BlockSpec((1, D_MODEL), lambda bi, i: (0, 0))
    weight = pl.BlockSpec((D_MODEL, D_MODEL), lambda bi, i: (0, 0), pipeline_mode=pl.Buffered(1))
    return pl.pallas_call(
        _attn_kernel,
        grid=(b, s // tq),
        in_specs=[
            pl.BlockSpec(memory_space=pltpu.SMEM),
            pl.BlockSpec((1, tq, D_MODEL), tile3),
            pl.BlockSpec((1, tq, D_MODEL), tile3),
            pl.BlockSpec((1, BLOCK, 2 * KV_WIDTH), lambda bi, i: (bi, jnp.maximum(i * r - 1, 0), 0)),
            pl.BlockSpec((1, tq, 2 * KV_WIDTH), tile3),
            pl.BlockSpec((1, BLOCK, 2 * KV_WIDTH), lambda bi, i: (bi, jnp.minimum((i + 1) * r, last_block), 0)),
            pl.BlockSpec((N_HEADS, 3 * BLOCK, BLOCK), lambda bi, i: (0, 0, 0), pipeline_mode=pl.Buffered(1)),
            vec,
            weight,
            weight,
            vec,
        ],
        out_specs=[pl.BlockSpec((1, tq, D_MODEL), tile3), pl.BlockSpec((1, tq, D_MODEL), tile3)],
        out_shape=[jax.ShapeDtypeStruct((b, s, D_MODEL), F32), jax.ShapeDtypeStruct((b, s, D_MODEL), BF16)],
        scratch_shapes=[
            pltpu.VMEM((tq + 2 * BLOCK, 2 * KV_WIDTH), BF16),
            pltpu.VMEM((tq + 2 * BLOCK, 2 * N_KV_HEADS * 2 * HEAD_DIM), BF16),
            pltpu.VMEM((N_KV_HEADS * VT_ROWS, tq + 2 * BLOCK), BF16),
            pltpu.VMEM((2, N_HEADS // 2, 3 * BLOCK, 2 * BLOCK), F32),
            pltpu.VMEM((2, N_HEADS // 2, F32_SUBLANES, 2 * BLOCK), F32),
            pltpu.VMEM((tq, D_MODEL), F32),
            pltpu.VMEM((tq, D_MODEL), F32),
        ],
        compiler_params=_params("parallel", "parallel", vmem=ATTN_VMEM_LIMIT),
        name="attention",
    )(sink, q, x, kv, kv, kv, bias, ga, wg, wo, gf)


def _stage1_kernel(w_ref, cs_ref, h_ref, y_ref):
    _, n1, tn2, d = h_ref.shape
    hs = jnp.swapaxes(h_ref[0], 0, 1).reshape(tn2 * n1, d)

    def channel_dft(grp):
        lo, hi = grp * FGROUP_CH, (grp + 1) * FGROUP_CH
        return jnp.dot(hs[:, lo:hi], cs_ref[...], preferred_element_type=F32).astype(BF16)

    t_next = channel_dft(0)
    for grp in range(N_FGROUPS):
        t = t_next
        ys = []
        for j in range(tn2):
            tj = t[j * n1:(j + 1) * n1]
            u = jnp.concatenate([tj[:, :FGROUP_CH], tj[:, FGROUP_CH:]], axis=0)
            ys.append(jnp.dot(w_ref[...], u, preferred_element_type=F32))
        if grp + 1 < N_FGROUPS:
            t_next = channel_dft(grp + 1)
        pairs = [pltpu.pack_elementwise([ys[j], ys[j + 1]], packed_dtype=BF16) for j in range(0, tn2, 2)]
        y = pltpu.bitcast(jnp.swapaxes(jnp.stack(pairs, axis=0), 0, 1), BF16)
        y_ref[0, :, :, :, grp * FGROUP_CH:(grp + 1) * FGROUP_CH] = y.reshape(2, n1, tn2, FGROUP_CH)


def _dft_stage1(w1, cs, h4):
    b, n1, n2, d = h4.shape
    const = lambda bi, i: (0, 0)
    return pl.pallas_call(
        _stage1_kernel,
        grid=(b, n2 // BF16_SUBLANES),
        in_specs=[
            pl.BlockSpec((2 * n1, 2 * n1), const),
            pl.BlockSpec((FGROUP_CH, 2 * FGROUP_CH), const),
            pl.BlockSpec((1, n1, BF16_SUBLANES, d), lambda bi, i: (bi, 0, i, 0)),
        ],
        out_specs=pl.BlockSpec((1, 2, n1, BF16_SUBLANES, d), lambda bi, i: (bi, 0, 0, i, 0)),
        out_shape=jax.ShapeDtypeStruct((b, 2, n1, n2, d), BF16),
        compiler_params=_params("parallel", "parallel"),
        name="dft_stage1",
    )(w1, cs, h4)


def _stage2_kernel(y_ref, m_ref, x_ref, gf_ref, wg_ref, wo_ref, gl_ref, out_ref, f_ref):
    tk1, n2 = m_ref.shape[0], m_ref.shape[1]
    fs = []
    for j in range(tk1):
        ycat = jnp.concatenate([y_ref[0, 0, j], y_ref[0, 1, j]], axis=0)
        fs.append(jnp.dot(m_ref[j], ycat, preferred_element_type=F32))
    f_ref[...] = jnp.swapaxes(jnp.stack(fs, axis=0), 0, 1)
    ck2 = STAGE2_CHUNK // tk1
    for c0 in range(0, n2, ck2):
        x = x_ref[0, c0:c0 + ck2].reshape(STAGE2_CHUNK, D_MODEL)
        h = _rms_scale(x, gf_ref[...]).astype(BF16)
        sg = _silu(jnp.dot(h, wg_ref[...], preferred_element_type=F32))
        fg = (f_ref[c0:c0 + ck2].reshape(STAGE2_CHUNK, D_MODEL) * sg).astype(BF16)
        xr = x + jnp.dot(fg, wo_ref[...], preferred_element_type=F32)
        out_ref[0, c0:c0 + ck2] = _rms_scale(xr, gl_ref[...]).reshape(ck2, tk1, D_MODEL)


def _dft_stage2(y5, mtab, x4, gf, wg, wo, gl):
    b, _, n1, n2, d = y5.shape
    tk1 = STAGE2_ROWS // n2
    rows = pl.BlockSpec((1, n2, tk1, d), lambda bi, i: (bi, 0, i, 0))
    const = lambda bi, i: (0, 0)
    return pl.pallas_call(
        _stage2_kernel,
        grid=(b, n1 // tk1),
        in_specs=[
            pl.BlockSpec((1, 2, tk1, n2, d), lambda bi, i: (bi, 0, i, 0, 0)),
            pl.BlockSpec((tk1, n2, 2 * n2), lambda bi, i: (i, 0, 0)),
            rows,
            pl.BlockSpec((1, d), const),
            pl.BlockSpec((d, d), const),
            pl.BlockSpec((d, d), const),
            pl.BlockSpec((1, d), const),
        ],
        out_specs=rows,
        out_shape=jax.ShapeDtypeStruct(x4.shape, F32),
        scratch_shapes=[pltpu.VMEM((n2, tk1, d), F32)],
        compiler_params=_params("parallel", "parallel"),
        name="dft_stage2",
    )(y5, mtab, x4, gf, wg, wo, gl)


def _trunk(x, bias, attn_norm, wq, wkv, wgate_a, wo_a, sink, fourier_norm, wgate_f, wo_f, final_norm, cs):
    b, s, d = x.shape
    n2 = s // DFT_N1
    q, kv = _inproj(x.reshape(b * s, d), attn_norm, wq, wkv)
    x1, hn = _attention(sink, q.reshape(b, s, d), kv.reshape(b, s, 2 * KV_WIDTH), x, bias,
                        attn_norm, wgate_a, wo_a, fourier_norm)
    w1 = jnp.asarray(_stage1_table(s)).astype(BF16)
    mtab = jnp.asarray(_stage2_table(s)).astype(BF16)
    y5 = _dft_stage1(w1, cs, hn.reshape(b, DFT_N1, n2, d))
    out = _dft_stage2(y5, mtab, x1.reshape(b, n2, DFT_N1, d), fourier_norm, wgate_f, wo_f, final_norm)
    return out.reshape(b, s, d)


def kernel(x_prompt, x_sample, rel_bias, attn_norm, attn_w_in, attn_w_out, attn_sink,
           fourier_norm, fourier_w_gate, fourier_w_out, final_norm):
    bucket, band_mask = _band_tables()
    onehot = (jnp.asarray(bucket.T.reshape(-1))[:, None] == jnp.arange(NUM_BUCKETS)[None, :]).astype(F32)
    bias = jnp.einsum("pb,bh->hp", onehot, rel_bias.astype(F32), precision=lax.Precision.HIGHEST)
    bias = bias.reshape(N_HEADS, 3 * BLOCK, BLOCK) * LOG2E + jnp.asarray(band_mask.T)[None]
    w_in = attn_w_in[0]
    wq = w_in[:, :D_MODEL].astype(BF16)
    wkv = w_in[:, D_MODEL:D_MODEL + 2 * KV_WIDTH].astype(BF16)
    wgate_a = w_in[:, D_MODEL + 2 * KV_WIDTH:].astype(BF16)
    cs = jnp.asarray(_channel_dft_table()).astype(BF16)
    args = (bias, attn_norm[0][None], wq, wkv, wgate_a, attn_w_out[0].astype(BF16), attn_sink[0] * LOG2E,
            fourier_norm[0][None], fourier_w_gate[0].astype(BF16), fourier_w_out[0].astype(BF16),
            final_norm[None], cs)
    return (_trunk(x_prompt, *args), _trunk(x_sample, *args))
```

```python
import functools
import math

import jax
import jax.numpy as jnp
import numpy as np
from jax import lax
from jax.experimental import pallas as pl
from jax.experimental.pallas import tpu as pltpu

D_MODEL = 1024
HEAD_DIM = 64
N_HEADS = 16
N_KV_HEADS = 4
GROUP = 4
KV_WIDTH = N_KV_HEADS * HEAD_DIM
WINDOW = 128
BLOCK = 128
NUM_BUCKETS = 32
MAX_DISTANCE = 128
N_FGROUPS = 4
FGROUP_CH = 256
RMS_EPS = 1e-6
MASK_VALUE = -1e30
LOG2E = math.log2(math.e)

DFT_N1 = 128
ROW_TILE = 1024
ATTN_ROWS = 1024
VT_ROWS = HEAD_DIM + 16
PROJ_BLOCKS = 2
BF16_SUBLANES = 16
F32_SUBLANES = 8
STAGE2_ROWS = 1024
STAGE2_CHUNK = 512
VMEM_LIMIT = 48 * 1024 * 1024
ATTN_VMEM_LIMIT = 58 * 1024 * 1024

BF16 = jnp.bfloat16
F32 = jnp.float32


def _params(*sem, vmem=VMEM_LIMIT):
    return pltpu.CompilerParams(dimension_semantics=sem, vmem_limit_bytes=vmem)


def _rms_scale(x, g):
    inv = lax.rsqrt(jnp.mean(x * x, axis=-1, keepdims=True) + RMS_EPS)
    return x * inv * g


def _silu(z):
    hz = 0.5 * z
    return hz + hz * jnp.tanh(hz)


def _t5_bucket_np(rel):
    half = NUM_BUCKETS // 2
    n = -rel
    ret = (n < 0).astype(np.int32) * half
    n = np.abs(n)
    max_exact = half // 2
    is_small = n < max_exact
    large = max_exact + (np.log(np.maximum(n, 1) / max_exact) / math.log(MAX_DISTANCE / max_exact)
                         * (half - max_exact)).astype(np.int32)
    large = np.minimum(large, half - 1)
    return (ret + np.where(is_small, n, large)).astype(np.int32)


@functools.lru_cache(maxsize=None)
def _band_tables():
    qi = np.arange(BLOCK)[:, None]
    kj = np.arange(3 * BLOCK)[None, :]
    rel = kj - BLOCK - qi
    band = np.abs(rel) <= WINDOW
    return _t5_bucket_np(rel), np.where(band, 0.0, MASK_VALUE).astype(np.float32)


@functools.lru_cache(maxsize=None)
def _channel_dft_table():
    c = np.arange(FGROUP_CH)
    ang = 2.0 * np.pi * ((c[:, None] * c[None, :]) % FGROUP_CH) / FGROUP_CH
    return (np.concatenate([np.cos(ang), np.sin(ang)], axis=1) / 16.0).astype(np.float32)


@functools.lru_cache(maxsize=None)
def _stage1_table(seq):
    k = np.arange(DFT_N1)
    ang = 2.0 * np.pi * ((k[:, None] * k[None, :]) % DFT_N1) / DFT_N1
    c, s = np.cos(ang), np.sin(ang)
    w = np.block([[c, -s], [-s, -c]]) / math.sqrt(seq)
    return w.astype(np.float32)


@functools.lru_cache(maxsize=None)
def _stage2_table(seq):
    n2_len = seq // DFT_N1
    k1 = np.arange(DFT_N1)[:, None, None]
    k2 = np.arange(n2_len)[None, :, None]
    n2 = np.arange(n2_len)[None, None, :]
    ang = 2.0 * np.pi * (((k1 + DFT_N1 * k2) * n2) % seq) / seq
    return np.concatenate([np.cos(ang), np.sin(ang)], axis=2).astype(np.float32)


def _inproj_kernel(x_ref, g_ref, wq_ref, wkv_ref, q_ref, kv_ref):
    h = _rms_scale(x_ref[...], g_ref[...]).astype(BF16)
    q = jnp.dot(h, wq_ref[...], preferred_element_type=F32)
    q_ref[...] = (q * (HEAD_DIM ** -0.5 * LOG2E)).astype(BF16)
    kv_ref[...] = jnp.dot(h, wkv_ref[...], preferred_element_type=F32).astype(BF16)


def _inproj(x2, g, wq, wkv):
    t = x2.shape[0]
    row = lambda i: (i, 0)
    const = lambda i: (0, 0)
    return pl.pallas_call(
        _inproj_kernel,
        grid=(t // ROW_TILE,),
        in_specs=[
            pl.BlockSpec((ROW_TILE, D_MODEL), row),
            pl.BlockSpec((1, D_MODEL), const),
            pl.BlockSpec((D_MODEL, D_MODEL), const),
            pl.BlockSpec((D_MODEL, 2 * KV_WIDTH), const),
        ],
        out_specs=[
            pl.BlockSpec((ROW_TILE, D_MODEL), row),
            pl.BlockSpec((ROW_TILE, 2 * KV_WIDTH), row),
        ],
        out_shape=[
            jax.ShapeDtypeStruct((t, D_MODEL), BF16),
            jax.ShapeDtypeStruct((t, 2 * KV_WIDTH), BF16),
        ],
        compiler_params=_params("parallel"),
        name="inproj",
    )(x2, g, wq, wkv)


def _attn_kernel(sink_ref, q_ref, x_ref, kvp_ref, kvc_ref, kvn_ref, bias_ref, ga_ref, wg_ref, wo_ref,
                 out_ref, kvw_ref, kz_ref, vt_ref, s_ref, m_ref, o_ref, sg_ref):
    tq = q_ref.shape[1]
    blocks_per_tile = tq // BLOCK
    tile = pl.program_id(1)

    kvw_ref[0:BLOCK, :] = kvp_ref[0]
    kvw_ref[BLOCK:BLOCK + tq, :] = kvc_ref[0]
    kvw_ref[BLOCK + tq:, :] = kvn_ref[0]

    low = lax.broadcasted_iota(jnp.int32, (1, 2 * HEAD_DIM), 1) < HEAD_DIM
    for c in range(N_KV_HEADS // 2):
        kc = kvw_ref[:, 2 * c * HEAD_DIM:(2 * c + 2) * HEAD_DIM]
        kc_sw = jnp.concatenate([kc[:, HEAD_DIM:], kc[:, :HEAD_DIM]], axis=1)
        zero = jnp.zeros_like(kc)
        for t, tab in enumerate((jnp.where(low, kc, zero), jnp.where(low, zero, kc_sw),
                                 jnp.where(low, kc_sw, zero), jnp.where(low, zero, kc))):
            kz_ref[:, (4 * c + t) * 2 * HEAD_DIM:(4 * c + t + 1) * 2 * HEAD_DIM] = tab
    v_t = kvw_ref[:, KV_WIDTH:].astype(F32).T.astype(BF16)
    for hh in range(N_KV_HEADS):
        vt_ref[hh * VT_ROWS:hh * VT_ROWS + HEAD_DIM] = v_t[hh * HEAD_DIM:(hh + 1) * HEAD_DIM]
        vt_ref[hh * VT_ROWS + HEAD_DIM:(hh + 1) * VT_ROWS] = jnp.ones((VT_ROWS - HEAD_DIM, tq + 2 * BLOCK), BF16)

    nt_dims = (((1,), (1,)), ((), ()))
    first_half = lax.broadcasted_iota(jnp.int32, (1, 2 * BLOCK), 1) < BLOCK

    def sink_lanes(hh, par):
        return jnp.where(first_half, sink_ref[hh * GROUP + par], sink_ref[hh * GROUP + 2 + par])

    def scores(j, hh):
        slot, r0 = j % 2, j * BLOCK
        qp = q_ref[0, pl.ds(r0, BLOCK), hh * 4 * HEAD_DIM:(hh + 1) * 4 * HEAD_DIM]
        qs = jnp.concatenate([qp[:, :2 * HEAD_DIM], qp[:, 2 * HEAD_DIM:]], axis=0)
        kz2 = jnp.concatenate(
            [kz_ref[pl.ds(r0, 3 * BLOCK), (2 * hh + par) * 2 * HEAD_DIM:(2 * hh + par + 1) * 2 * HEAD_DIM]
             for par in range(2)], axis=0)
        s2 = lax.dot_general(kz2, qs, nt_dims, preferred_element_type=F32)
        for par in range(2):
            u = 2 * hh + par
            bias = jnp.concatenate([bias_ref[hh * GROUP + par], bias_ref[hh * GROUP + 2 + par]], axis=1)
            s = s2[par * 3 * BLOCK:(par + 1) * 3 * BLOCK] + bias
            if j == 0:
                s = jnp.concatenate([s[:BLOCK] + jnp.where(tile == 0, MASK_VALUE, 0.0), s[BLOCK:]], axis=0)
            if j == blocks_per_tile - 1:
                last = jnp.where(tile == pl.num_programs(1) - 1, MASK_VALUE, 0.0)
                s = jnp.concatenate([s[:2 * BLOCK], s[2 * BLOCK:] + last], axis=0)
            s_ref[slot, u] = s
            m = jnp.maximum(jnp.max(s, axis=0, keepdims=True), sink_lanes(hh, par))
            m_ref[slot, u] = jnp.broadcast_to(m, (F32_SUBLANES, 2 * BLOCK))

    def finish(j, hh):
        slot, r0 = j % 2, j * BLOCK
        vt = vt_ref[hh * VT_ROWS:(hh + 1) * VT_ROWS, pl.ds(r0, 3 * BLOCK)]
        outs = []
        for par in range(2):
            u = 2 * hh + par
            m = m_ref[slot, u][0:1]
            p = jnp.exp2(s_ref[slot, u] - m)
            o_t = jnp.dot(vt, p.astype(BF16), preferred_element_type=F32)
            l = o_t[HEAD_DIM:HEAD_DIM + 1] + jnp.exp2(sink_lanes(hh, par) - m)
            outs.append(o_t[:HEAD_DIM] * (1.0 / l))
        for t in range(2):
            pair_t = jnp.concatenate([outs[0][:, t * BLOCK:(t + 1) * BLOCK],
                                      outs[1][:, t * BLOCK:(t + 1) * BLOCK]], axis=0)
            pair = 2 * hh + t
            o_ref[pl.ds(r0, BLOCK), pair * 2 * HEAD_DIM:(pair + 1) * 2 * HEAD_DIM] = pair_t.T

    def project(r0, rows):
        og = (o_ref[r0:r0 + rows] * sg_ref[r0:r0 + rows]).astype(BF16)
        out_ref[0, r0:r0 + rows] = x_ref[0, r0:r0 + rows] + jnp.dot(og, wo_ref[...], preferred_element_type=F32)

    def gate(r0, rows):
        h = _rms_scale(x_ref[0, r0:r0 + rows], ga_ref[...]).astype(BF16)
        sg_ref[r0:r0 + rows] = _silu(jnp.dot(h, wg_ref[...], preferred_element_type=F32))

    for hh in range(N_KV_HEADS):
        scores(0, hh)
    for j in range(blocks_per_tile):
        if j % PROJ_BLOCKS == 0:
            gate(j * BLOCK, PROJ_BLOCKS * BLOCK)
        for hh in range(N_KV_HEADS):
            if j + 1 < blocks_per_tile:
                scores(j + 1, hh)
            finish(j, hh)
        if (j + 1) % PROJ_BLOCKS == 0:
            project((j + 1 - PROJ_BLOCKS) * BLOCK, PROJ_BLOCKS * BLOCK)


def _attention(sink, q, kv, x, bias, ga, wg, wo):
    b, s, _ = x.shape
    tq = ATTN_ROWS
    r = tq // BLOCK
    last_block = s // BLOCK - 1
    tile3 = lambda bi, i: (bi, i, 0)
    vec = pl.BlockSpec((1, D_MODEL), lambda bi, i: (0, 0))
    weight = pl.BlockSpec((D_MODEL, D_MODEL), lambda bi, i: (0, 0), pipeline_mode=pl.Buffered(1))
    return pl.pallas_call(
        _attn_kernel,
        grid=(b, s // tq),
        in_specs=[
            pl.BlockSpec(memory_space=pltpu.SMEM),
            pl.BlockSpec((1, tq, D_MODEL), tile3),
            pl.BlockSpec((1, tq, D_MODEL), tile3),
            pl.BlockSpec((1, BLOCK, 2 * KV_WIDTH), lambda bi, i: (bi, jnp.maximum(i * r - 1, 0), 0)),
            pl.BlockSpec((1, tq, 2 * KV_WIDTH), tile3),
            pl.BlockSpec((1, BLOCK, 2 * KV_WIDTH), lambda bi, i: (bi, jnp.minimum((i + 1) * r, last_block), 0)),
            pl.BlockSpec((N_HEADS, 3 * BLOCK, BLOCK), lambda bi, i: (0, 0, 0), pipeline_mode=pl.Buffered(1)),
            vec,
            weight,
            weight,
        ],
        out_specs=pl.BlockSpec((1, tq, D_MODEL), tile3),
        out_shape=jax.ShapeDtypeStruct((b, s, D_MODEL), F32),
        scratch_shapes=[
            pltpu.VMEM((tq + 2 * BLOCK, 2 * KV_WIDTH), BF16),
            pltpu.VMEM((tq + 2 * BLOCK, 2 * N_KV_HEADS * 2 * HEAD_DIM), BF16),
            pltpu.VMEM((N_KV_HEADS * VT_ROWS, tq + 2 * BLOCK), BF16),
            pltpu.VMEM((2, N_HEADS // 2, 3 * BLOCK, 2 * BLOCK), F32),
            pltpu.VMEM((2, N_HEADS // 2, F32_SUBLANES, 2 * BLOCK), F32),
            pltpu.VMEM((tq, D_MODEL), F32),
            pltpu.VMEM((tq, D_MODEL), F32),
        ],
        compiler_params=_params("parallel", "parallel", vmem=ATTN_VMEM_LIMIT),
        name="attention",
    )(sink, q, x, kv, kv, kv, bias, ga, wg, wo)


def _stage1_kernel(w_ref, cs_ref, g_ref, x_ref, y_ref):
    _, n1, tn2, d = x_ref.shape
    h = _rms_scale(x_ref[0].reshape(n1 * tn2, d), g_ref[...]).astype(BF16).reshape(n1, tn2, d)
    hs = jnp.swapaxes(h, 0, 1).reshape(tn2 * n1, d)

    def channel_dft(grp):
        lo, hi = grp * FGROUP_CH, (grp + 1) * FGROUP_CH
        return jnp.dot(hs[:, lo:hi], cs_ref[...], preferred_element_type=F32).astype(BF16)

    t_next = channel_dft(0)
    for grp in range(N_FGROUPS):
        t = t_next
        ys = []
        for j in range(tn2):
            tj = t[j * n1:(j + 1) * n1]
            u = jnp.concatenate([tj[:, :FGROUP_CH], tj[:, FGROUP_CH:]], axis=0)
            ys.append(jnp.dot(w_ref[...], u, preferred_element_type=F32))
        if grp + 1 < N_FGROUPS:
            t_next = channel_dft(grp + 1)
        pairs = [pltpu.pack_elementwise([ys[j], ys[j + 1]], packed_dtype=BF16) for j in range(0, tn2, 2)]
        y = pltpu.bitcast(jnp.swapaxes(jnp.stack(pairs, axis=0), 0, 1), BF16)
        y_ref[0, :, :, :, grp * FGROUP_CH:(grp + 1) * FGROUP_CH] = y.reshape(2, n1, tn2, FGROUP_CH)


def _dft_stage1(w1, cs, g, x4):
    b, n1, n2, d = x4.shape
    const = lambda bi, i: (0, 0)
    return pl.pallas_call(
        _stage1_kernel,
        grid=(b, n2 // BF16_SUBLANES),
        in_specs=[
            pl.BlockSpec((2 * n1, 2 * n1), const),
            pl.BlockSpec((FGROUP_CH, 2 * FGROUP_CH), const),
            pl.BlockSpec((1, d), const),
            pl.BlockSpec((1, n1, BF16_SUBLANES, d), lambda bi, i: (bi, 0, i, 0)),
        ],
        out_specs=pl.BlockSpec((1, 2, n1, BF16_SUBLANES, d), lambda bi, i: (bi, 0, 0, i, 0)),
        out_shape=jax.ShapeDtypeStruct((b, 2, n1, n2, d), BF16),
        compiler_params=_params("parallel", "parallel", vmem=ATTN_VMEM_LIMIT),
        name="dft_stage1",
    )(w1, cs, g, x4)


def _stage2_kernel(y_ref, m_ref, x_ref, gf_ref, wg_ref, wo_ref, gl_ref, out_ref, f_ref):
    tk1, n2 = m_ref.shape[0], m_ref.shape[1]
    fs = []
    for j in range(tk1):
        ycat = jnp.concatenate([y_ref[0, 0, j], y_ref[0, 1, j]], axis=0)
        fs.append(jnp.dot(m_ref[j], ycat, preferred_element_type=F32))
    f_ref[...] = jnp.swapaxes(jnp.stack(fs, axis=0), 0, 1)
    ck2 = STAGE2_CHUNK // tk1
    for c0 in range(0, n2, ck2):
        x = x_ref[0, c0:c0 + ck2].reshape(STAGE2_CHUNK, D_MODEL)
        h = _rms_scale(x, gf_ref[...]).astype(BF16)
        sg = _silu(jnp.dot(h, wg_ref[...], preferred_element_type=F32))
        fg = (f_ref[c0:c0 + ck2].reshape(STAGE2_CHUNK, D_MODEL) * sg).astype(BF16)
        xr = x + jnp.dot(fg, wo_ref[...], preferred_element_type=F32)
        out_ref[0, c0:c0 + ck2] = _rms_scale(xr, gl_ref[...]).reshape(ck2, tk1, D_MODEL)


def _dft_stage2(y5, mtab, x4, gf, wg, wo, gl):
    b, _, n1, n2, d = y5.shape
    tk1 = STAGE2_ROWS // n2
    rows = pl.BlockSpec((1, n2, tk1, d), lambda bi, i: (bi, 0, i, 0))
    const = lambda bi, i: (0, 0)
    return pl.pallas_call(
        _stage2_kernel,
        grid=(b, n1 // tk1),
        in_specs=[
            pl.BlockSpec((1, 2, tk1, n2, d), lambda bi, i: (bi, 0, i, 0, 0)),
            pl.BlockSpec((tk1, n2, 2 * n2), lambda bi, i: (i, 0, 0)),
            rows,
            pl.BlockSpec((1, d), const),
            pl.BlockSpec((d, d), const),
            pl.BlockSpec((d, d), const),
            pl.BlockSpec((1, d), const),
        ],
        out_specs=rows,
        out_shape=jax.ShapeDtypeStruct(x4.shape, F32),
        scratch_shapes=[pltpu.VMEM((n2, tk1, d), F32)],
        compiler_params=_params("parallel", "parallel"),
        name="dft_stage2",
    )(y5, mtab, x4, gf, wg, wo, gl)


def _trunk(x, bias, attn_norm, wq, wkv, wgate_a, wo_a, sink, fourier_norm, wgate_f, wo_f, final_norm, cs):
    b, s, d = x.shape
    n2 = s // DFT_N1
    q, kv = _inproj(x.reshape(b * s, d), attn_norm, wq, wkv)
    x1 = _attention(sink, q.reshape(b, s, d), kv.reshape(b, s, 2 * KV_WIDTH), x, bias, attn_norm, wgate_a, wo_a)
    w1 = jnp.asarray(_stage1_table(s)).astype(BF16)
    mtab = jnp.asarray(_stage2_table(s)).astype(BF16)
    y5 = _dft_stage1(w1, cs, fourier_norm, x1.reshape(b, DFT_N1, n2, d))
    out = _dft_stage2(y5, mtab, x1.reshape(b, n2, DFT_N1, d), fourier_norm, wgate_f, wo_f, final_norm)
    return out.reshape(b, s, d)


def kernel(x_prompt, x_sample, rel_bias, attn_norm, attn_w_in, attn_w_out, attn_sink,
           fourier_norm, fourier_w_gate, fourier_w_out, final_norm):
    bucket, band_mask = _band_tables()
    onehot = (jnp.asarray(bucket.T.reshape(-1))[:, None] == jnp.arange(NUM_BUCKETS)[None, :]).astype(F32)
    bias = jnp.einsum("pb,bh->hp", onehot, rel_bias.astype(F32), precision=lax.Precision.HIGHEST)
    bias = bias.reshape(N_HEADS, 3 * BLOCK, BLOCK) * LOG2E + jnp.asarray(band_mask.T)[None]
    w_in = attn_w_in[0]
    wq = w_in[:, :D_MODEL].astype(BF16)
    wkv = w_in[:, D_MODEL:D_MODEL + 2 * KV_WIDTH].astype(BF16)
    wgate_a = w_in[:, D_MODEL + 2 * KV_WIDTH:].astype(BF16)
    cs = jnp.asarray(_channel_dft_table()).astype(BF16)
    args = (bias, attn_norm[0][None], wq, wkv, wgate_a, attn_w_out[0].astype(BF16), attn_sink[0] * LOG2E,
            fourier_norm[0][None], fourier_w_gate[0].astype(BF16), fourier_w_out[0].astype(BF16),
            final_norm[None], cs)
    return (_trunk(x_prompt, *args), _trunk(x_sample, *args))
```
